```python
import jax, jax.numpy as jnp
from jax import lax
import numpy as np

D_MODEL = 2048
BATCH = 2
SEQ = 8192
DEPTH = 4
DEC_BATCH = 2
DEC_SEQ = 4096
PAST_LEN = 128

N_MIXERS = 2
N_A_LAYERS = (DEPTH + 1) // 2
N_B_LAYERS = DEPTH // 2
CHUNK = 128
A_WIDTH = D_MODEL
A_GROUPS = 8
A_GROUP_DIM = A_WIDTH // A_GROUPS
HEAD_DIM = 128
N_HEADS = D_MODEL // HEAD_DIM
N_KV_HEADS = 4
Q_PER_KV = N_HEADS // N_KV_HEADS
WINDOW = 128
BLOCK = 128
N_BUCKETS = 32
MAX_DISTANCE = 128
D_FF = ((8 * D_MODEL // 3) + 255) // 256 * 256
CONV_WIDTH = 3
EPS = 1e-6

kernel_name = "hybrid_gmlp_swa_encoder"


def rmsnorm(x, g):
    xf = x.astype(jnp.float32)
    y = xf * lax.rsqrt(jnp.mean(xf * xf, axis=-1, keepdims=True) + EPS)
    return (y * g.astype(jnp.float32)).astype(x.dtype)


def _relative_bucket(rel):
    half = N_BUCKETS // 2
    max_exact = half // 2
    ret = (rel > 0).astype(np.int32) * half
    n = np.abs(rel)
    nf = np.maximum(n, 1).astype(np.float32)
    large = max_exact + (np.log(nf / max_exact) / np.log(MAX_DISTANCE / max_exact)
                         * (half - max_exact)).astype(np.int32)
    large = np.minimum(large, half - 1)
    return (ret + np.where(n < max_exact, n, large)).astype(np.int32)


def _band_bucket_mask(n_blocks, seq_len):
    a = np.arange(BLOCK)[:, None]
    c = np.arange(3 * BLOCK)[None, :]
    rel = c - BLOCK - a
    bucket = _relative_bucket(rel)
    in_window = np.abs(rel) <= WINDOW
    key_pos = np.arange(n_blocks)[:, None, None] * BLOCK + (c - BLOCK)[None]
    mask = (key_pos >= 0) & (key_pos < seq_len) & in_window[None]
    return bucket, mask


def mixer_a(h, w_in, b_in, v_norm, w_s, b_s, w_out):
    B, S, _ = h.shape
    nc = S // CHUNK
    z = jax.nn.gelu(h @ w_in + b_in)
    u, v = jnp.split(z, 2, axis=-1)
    v = rmsnorm(v, v_norm).reshape(B, nc, CHUNK, A_GROUPS, A_GROUP_DIM)
    s = jnp.einsum('gpq,bnqgc->bnpgc', w_s, v) + b_s.T[None, None, :, :, None]
    y = u * s.reshape(B, S, A_WIDTH)
    return y @ w_out


def mixer_b(h, rel_bias, w_qkv, sink, w_out):
    B, S, _ = h.shape
    nb = S // BLOCK
    qkv = h @ w_qkv
    q, k, v = jnp.split(qkv, [N_HEADS * HEAD_DIM, (N_HEADS + N_KV_HEADS) * HEAD_DIM], axis=-1)
    q = q.reshape(B, nb, BLOCK, N_KV_HEADS, Q_PER_KV, HEAD_DIM)

    def windows(t):
        t = t.reshape(B, S, N_KV_HEADS, HEAD_DIM)
        tp = jnp.pad(t, ((0, 0), (BLOCK, BLOCK), (0, 0), (0, 0)))
        tp = tp.reshape(B, nb + 2, BLOCK, N_KV_HEADS, HEAD_DIM)
        return jnp.concatenate([tp[:, :-2], tp[:, 1:-1], tp[:, 2:]], axis=2)

    kw, vw = windows(k), windows(v)
    bucket, mask = _band_bucket_mask(nb, S)
    bias = rel_bias.astype(jnp.float32)[bucket]
    bias = bias.transpose(2, 0, 1).reshape(N_KV_HEADS, Q_PER_KV, BLOCK, 3 * BLOCK)
    logits = jnp.einsum('bnqkgd,bnckd->bnkgqc', q, kw,
                        preferred_element_type=jnp.float32) * (HEAD_DIM ** -0.5) + bias
    logits = jnp.where(mask[None, :, None, None], logits, -jnp.inf)
    sink_l = sink.astype(jnp.float32).reshape(N_KV_HEADS, Q_PER_KV, 1, 1)
    m = jnp.maximum(jnp.max(logits, axis=-1, keepdims=True), sink_l)
    p = jnp.exp(logits - m)
    denom = jnp.sum(p, axis=-1, keepdims=True) + jnp.exp(sink_l - m)
    p = (p / denom).astype(vw.dtype)
    o = jnp.einsum('bnkgqc,bnckd->bnqkgd', p, vw).reshape(B, S, N_HEADS * HEAD_DIM)
    return o @ w_out


def conv_ffn(h, w_in, conv_w, conv_b, w_out):
    z = h @ w_in
    zp = jnp.pad(z, ((0, 0), (1, 1), (0, 0)))
    z = zp[:, :-2] * conv_w[0] + zp[:, 1:-1] * conv_w[1] + zp[:, 2:] * conv_w[2] + conv_b
    g, u = jnp.split(z, 2, axis=-1)
    return (jax.nn.silu(g) * u) @ w_out


def trunk(x, rel_bias, mix_norm, ffn_norm, final_norm,
          a_w_in, a_b_in, a_v_norm, a_w_s, a_b_s, a_w_out,
          b_w_qkv, b_sink, b_w_out,
          f_w_in, f_conv_w, f_conv_b, f_w_out):
    for i in range(DEPTH):
        h = rmsnorm(x, mix_norm[i])
        j = i // N_MIXERS
        if i % N_MIXERS == 0:
            x = x + mixer_a(h, a_w_in[j], a_b_in[j], a_v_norm[j], a_w_s[j], a_b_s[j], a_w_out[j])
        else:
            x = x + mixer_b(h, rel_bias, b_w_qkv[j], b_sink[j], b_w_out[j])
        h = rmsnorm(x, ffn_norm[i])
        x = x + conv_ffn(h, f_w_in[i], f_conv_w[i], f_conv_b[i], f_w_out[i])
    return rmsnorm(x, final_norm)


def setup_inputs(seed: int = 0) -> dict:
    key = jax.random.key(seed)
    ks = jax.random.split(key, 20)
    f32 = jnp.float32

    def nrm(k, shape, scale):
        return jax.random.normal(k, shape, f32) * scale

    qkv_dim = (N_HEADS + 2 * N_KV_HEADS) * HEAD_DIM
    return {
        "x_prompt": nrm(ks[0], (BATCH, SEQ, D_MODEL), 1.0),
        "x_sample": nrm(ks[1], (DEC_BATCH, DEC_SEQ, D_MODEL), 1.0),
        "rel_bias": nrm(ks[2], (N_BUCKETS, N_HEADS), 0.5),
        "mix_norm": 1.0 + nrm(ks[3], (DEPTH, D_MODEL), 0.02),
        "ffn_norm": 1.0 + nrm(ks[4], (DEPTH, D_MODEL), 0.02),
        "final_norm": 1.0 + nrm(ks[5], (D_MODEL,), 0.02),
        "a_w_in": nrm(ks[6], (N_A_LAYERS, D_MODEL, 2 * A_WIDTH), D_MODEL ** -0.5),
        "a_b_in": nrm(ks[7], (N_A_LAYERS, 2 * A_WIDTH), 0.02),
        "a_v_norm": 1.0 + nrm(ks[8], (N_A_LAYERS, A_WIDTH), 0.02),
        "a_w_s": nrm(ks[9], (N_A_LAYERS, A_GROUPS, CHUNK, CHUNK), CHUNK ** -0.5),
        "a_b_s": 1.0 + nrm(ks[10], (N_A_LAYERS, A_GROUPS, CHUNK), 0.02),
        "a_w_out": nrm(ks[11], (N_A_LAYERS, A_WIDTH, D_MODEL), A_WIDTH ** -0.5),
        "b_w_qkv": nrm(ks[12], (N_B_LAYERS, D_MODEL, qkv_dim), D_MODEL ** -0.5),
        "b_sink": nrm(ks[13], (N_B_LAYERS, N_HEADS), 0.5),
        "b_w_out": nrm(ks[14], (N_B_LAYERS, N_HEADS * HEAD_DIM, D_MODEL), (N_HEADS * HEAD_DIM) ** -0.5),
        "f_w_in": nrm(ks[15], (DEPTH, D_MODEL, 2 * D_FF), D_MODEL ** -0.5),
        "f_conv_w": nrm(ks[16], (DEPTH, CONV_WIDTH, 2 * D_FF), CONV_WIDTH ** -0.5),
        "f_conv_b": nrm(ks[17], (DEPTH, 2 * D_FF), 0.02),
        "f_w_out": nrm(ks[18], (DEPTH, D_FF, D_MODEL), D_FF ** -0.5),
    }


def reference(x_prompt, x_sample, rel_bias, mix_norm, ffn_norm, final_norm,
              a_w_in, a_b_in, a_v_norm, a_w_s, a_b_s, a_w_out,
              b_w_qkv, b_sink, b_w_out,
              f_w_in, f_conv_w, f_conv_b, f_w_out):
    y_prompt = trunk(x_prompt, rel_bias, mix_norm, ffn_norm, final_norm,
                     a_w_in, a_b_in, a_v_norm, a_w_s, a_b_s, a_w_out,
                     b_w_qkv, b_sink, b_w_out,
                     f_w_in, f_conv_w, f_conv_b, f_w_out)
    y_sample = trunk(x_sample, rel_bias, mix_norm, ffn_norm, final_norm,
                     a_w_in, a_b_in, a_v_norm, a_w_s, a_b_s, a_w_out,
                     b_w_qkv, b_sink, b_w_out,
                     f_w_in, f_conv_w, f_conv_b, f_w_out)
    return (y_prompt, y_sample)
```

```python
import functools

import numpy as np
import jax
import jax.numpy as jnp
from jax import lax
from jax.experimental import pallas as pl
from jax.experimental.pallas import tpu as pltpu

F32 = jnp.float32
BF16 = jnp.bfloat16

D_MODEL = 2048
DEPTH = 4
CHUNK = 128
A_GROUPS = 8
A_GROUP_DIM = D_MODEL // A_GROUPS
HEAD_DIM = 128
N_HEADS = D_MODEL // HEAD_DIM
N_KV_HEADS = 4
Q_PER_KV = N_HEADS // N_KV_HEADS
KV_DIM = N_KV_HEADS * HEAD_DIM
WINDOW = 128
BLOCK = 128
N_BUCKETS = 32
MAX_DISTANCE = 128
D_FF = 5632
EPS = 1e-6

VMEM_LIMIT_BYTES = 56 * 1024 * 1024
BF16_SUBLANES = 16
HALO = BF16_SUBLANES
NORM_ROWS = 64


def _params(*sem):
    return pltpu.CompilerParams(dimension_semantics=sem,
                                vmem_limit_bytes=VMEM_LIMIT_BYTES)


def _rmsnorm_val(x, g):
    ms = jnp.mean(x * x, axis=-1, keepdims=True)
    return (x * lax.rsqrt(ms + EPS)) * g


def _rmsnorm_rows(x_ref, g_ref, dst_ref, dst_off, n_rows):
    g = g_ref[...]

    def body(c, carry):
        r = pl.multiple_of(c * NORM_ROWS, NORM_ROWS)
        x = x_ref[pl.ds(r, NORM_ROWS), :]
        dst_ref[pl.ds(dst_off + r, NORM_ROWS), :] = _rmsnorm_val(x, g).astype(dst_ref.dtype)
        return carry

    lax.fori_loop(0, n_rows // NORM_ROWS, body, 0)


def _norm_proj_gelu_kernel(x_ref, g_ref, w_ref, b_ref, o_ref, h_ref):
    @pl.when(pl.program_id(1) == 0)
    def _():
        _rmsnorm_rows(x_ref, g_ref, h_ref, 0, x_ref.shape[0])

    acc = jnp.dot(h_ref[...], w_ref[...], preferred_element_type=F32)
    o_ref[...] = jax.nn.gelu(acc + b_ref[...]).astype(o_ref.dtype)


def _norm_proj_kernel(x_ref, g_ref, w_ref, o_ref, h_ref):
    @pl.when(pl.program_id(1) == 0)
    def _():
        _rmsnorm_rows(x_ref, g_ref, h_ref, 0, x_ref.shape[0])

    o_ref[...] = jnp.dot(h_ref[...], w_ref[...], preferred_element_type=F32).astype(o_ref.dtype)


def _norm_proj(x, gain, layer_g, w, layer_w, bias, *, out_dtype, tm, tn):
    t, d = x.shape
    n = w.shape[-1]
    in_specs = [
        pl.BlockSpec((tm, d), lambda i, j: (i, 0)),
        pl.BlockSpec((None, 1, d), lambda i, j: (layer_g, 0, 0)),
        pl.BlockSpec((None, d, tn), lambda i, j: (layer_w, 0, j)),
    ]
    args = [x, gain, w]
    if bias is not None:
        in_specs.append(pl.BlockSpec((None, 1, tn), lambda i, j: (layer_w, 0, j)))
        args.append(bias)
    return pl.pallas_call(
        _norm_proj_kernel if bias is None else _norm_proj_gelu_kernel,
        grid=(t // tm, n // tn),
        in_specs=in_specs,
        out_specs=pl.BlockSpec((tm, tn), lambda i, j: (i, j)),
        out_shape=jax.ShapeDtypeStruct((t, n), out_dtype),
        scratch_shapes=[pltpu.VMEM((tm, d), BF16)],
        compiler_params=_params("parallel", "arbitrary"),
        name="norm_proj" if bias is None else "norm_proj_gelu",
    )(*args)


def _gmlp_out_kernel(u_ref, v_ref, vg_ref, ws_ref, bs_ref, wo_ref, x_ref, o_ref,
                     vn_ref, y_ref):
    tm = u_ref.shape[0]
    _rmsnorm_rows(v_ref, vg_ref, vn_ref, 0, tm)
    for n in range(tm // CHUNK):
        rows = slice(n * CHUNK, (n + 1) * CHUNK)
        for g in range(A_GROUPS):
            cols = slice(g * A_GROUP_DIM, (g + 1) * A_GROUP_DIM)
            s = jnp.dot(ws_ref[g], vn_ref[rows, cols], preferred_element_type=F32)
            s = s + bs_ref[:, g:g + 1]
            y_ref[rows, cols] = (u_ref[rows, cols] * s).astype(BF16)
    o_ref[...] = x_ref[...] + jnp.dot(y_ref[...], wo_ref[...], preferred_element_type=F32)


def _gmlp_out(z, x, v_norm, w_s, b_s_t, w_out, layer, *, tm):
    t, d = x.shape
    return pl.pallas_call(
        _gmlp_out_kernel,
        grid=(t // tm,),
        in_specs=[
            pl.BlockSpec((tm, d), lambda i: (i, 0)),
            pl.BlockSpec((tm, d), lambda i: (i, 1)),
            pl.BlockSpec((None, 1, d), lambda i: (layer, 0, 0)),
            pl.BlockSpec((None, A_GROUPS, CHUNK, CHUNK), lambda i: (layer, 0, 0, 0)),
            pl.BlockSpec((None, CHUNK, A_GROUPS), lambda i: (layer, 0, 0)),
            pl.BlockSpec((None, d, d), lambda i: (layer, 0, 0)),
            pl.BlockSpec((tm, d), lambda i: (i, 0)),
        ],
        out_specs=pl.BlockSpec((tm, d), lambda i: (i, 0)),
        out_shape=jax.ShapeDtypeStruct((t, d), F32),
        scratch_shapes=[pltpu.VMEM((tm, d), BF16), pltpu.VMEM((tm, d), BF16)],
        compiler_params=_params("parallel"),
        name="gmlp_out",
    )(z, z, v_norm, w_s, b_s_t, w_out, x)


def _relative_bucket(rel):
    half = N_BUCKETS // 2
    max_exact = half // 2
    ret = (rel > 0).astype(np.int32) * half
    n = np.abs(rel)
    nf = np.maximum(n, 1).astype(np.float32)
    large = max_exact + (np.log(nf / max_exact) / np.log(MAX_DISTANCE / max_exact)
                         * (half - max_exact)).astype(np.int32)
    large = np.minimum(large, half - 1)
    return (ret + np.where(n < max_exact, n, large)).astype(np.int32)


def _band_bucket():
    a = np.arange(BLOCK)[:, None]
    c = np.arange(3 * BLOCK)[None, :]
    rel = c - BLOCK - a
    return np.where(np.abs(rel) <= WINDOW, _relative_bucket(rel), -1).astype(np.int32)


def _bias_table_kernel(bucket_ref, rb_ref, o_ref):
    bucket = bucket_ref[...]
    for h in range(N_HEADS):
        acc = jnp.full(bucket.shape, -jnp.inf, F32)
        for b in range(N_BUCKETS):
            acc = jnp.where(bucket == b, rb_ref[b, h], acc)
        o_ref[h] = acc


def _bias_table(rel_bias):
    return pl.pallas_call(
        _bias_table_kernel,
        in_specs=[pl.BlockSpec(memory_space=pltpu.VMEM),
                  pl.BlockSpec(memory_space=pltpu.SMEM)],
        out_specs=pl.BlockSpec(memory_space=pltpu.VMEM),
        out_shape=jax.ShapeDtypeStruct((N_HEADS, BLOCK, 3 * BLOCK), F32),
        name="bias_table",
    )(jnp.asarray(_band_bucket()), rel_bias)


def _attn_kernel(q_ref, kp_ref, kc_ref, kn_ref, vp_ref, vc_ref, vn_ref, bias_ref, sink_ref,
                 o_ref, *, blocks_per_seq, layer):
    blk = pl.program_id(0) % blocks_per_seq
    col = lax.broadcasted_iota(jnp.int32, (1, 3 * BLOCK), 1)
    first_col = jnp.where(blk == 0, BLOCK, 0)
    end_col = jnp.where(blk == blocks_per_seq - 1, 2 * BLOCK, 3 * BLOCK)
    edge = (col < first_col) | (col >= end_col)
    scale = HEAD_DIM ** -0.5
    for kh in range(N_KV_HEADS):
        kcols = slice(kh * HEAD_DIM, (kh + 1) * HEAD_DIM)
        k = jnp.concatenate([kp_ref[:, kcols], kc_ref[:, kcols], kn_ref[:, kcols]], axis=0)
        v = jnp.concatenate([vp_ref[:, kcols], vc_ref[:, kcols], vn_ref[:, kcols]], axis=0)
        for g in range(Q_PER_KV):
            h = kh * Q_PER_KV + g
            hcols = slice(h * HEAD_DIM, (h + 1) * HEAD_DIM)
            logits = lax.dot_general(q_ref[:, hcols], k, (((1,), (1,)), ((), ())),
                                     preferred_element_type=F32)
            logits = logits * scale + bias_ref[h]
            logits = jnp.where(edge, -jnp.inf, logits)
            sink = sink_ref[layer, h]
            m = jnp.maximum(jnp.max(logits, axis=-1, keepdims=True), sink)
            p = jnp.exp(logits - m)
            denom = jnp.sum(p, axis=-1, keepdims=True) + jnp.exp(sink - m)
            p = (p / denom).astype(BF16)
            o_ref[:, hcols] = jnp.dot(p, v, preferred_element_type=F32).astype(o_ref.dtype)


def _attention(qkv, bias, sink, layer, seq_len):
    t = qkv.shape[0]
    nblk = t // BLOCK
    bps = seq_len // BLOCK
    k0 = D_MODEL // KV_DIM
    v0 = k0 + 1

    def kv_spec(col, shift):
        return pl.BlockSpec((BLOCK, KV_DIM),
                            lambda i: (jnp.clip(i + shift, 0, nblk - 1), col))

    return pl.pallas_call(
        functools.partial(_attn_kernel, blocks_per_seq=bps, layer=layer),
        grid=(nblk,),
        in_specs=[
            pl.BlockSpec((BLOCK, D_MODEL), lambda i: (i, 0)),
            kv_spec(k0, -1), kv_spec(k0, 0), kv_spec(k0, 1),
            kv_spec(v0, -1), kv_spec(v0, 0), kv_spec(v0, 1),
            pl.BlockSpec((N_HEADS, BLOCK, 3 * BLOCK), lambda i: (0, 0, 0)),
            pl.BlockSpec(memory_space=pltpu.SMEM),
        ],
        out_specs=pl.BlockSpec((BLOCK, D_MODEL), lambda i: (i, 0)),
        out_shape=jax.ShapeDtypeStruct((t, D_MODEL), BF16),
        compiler_params=_params("parallel"),
        name="band_attention",
    )(qkv, qkv, qkv, qkv, qkv, qkv, qkv, bias, sink)


def _proj_residual_kernel(a_ref, w_ref, x_ref, o_ref):
    o_ref[...] = x_ref[...] + jnp.dot(a_ref[...], w_ref[...], preferred_element_type=F32)


def _proj_residual(a, w, layer, x, *, tm):
    t, d = x.shape
    k = a.shape[1]
    return pl.pallas_call(
        _proj_residual_kernel,
        grid=(t // tm,),
        in_specs=[
            pl.BlockSpec((tm, k), lambda i: (i, 0)),
            pl.BlockSpec((None, k, d), lambda i: (layer, 0, 0)),
            pl.BlockSpec((tm, d), lambda i: (i, 0)),
        ],
        out_specs=pl.BlockSpec((tm, d), lambda i: (i, 0)),
        out_shape=jax.ShapeDtypeStruct((t, d), F32),
        compiler_params=_params("parallel"),
        name="proj_residual",
    )(a, w, x)


def _ffn_kernel(x_ref, xp_ref, xn_ref, g_ref, wg_ref, wu_ref, cwg_ref, cwu_ref, cbg_ref,
                cbu_ref, wo_ref, o_ref, h_ref, zg_ref, zu_ref, *, tiles_per_seq):
    i = pl.program_id(0)
    c = pl.program_id(1)
    tm = x_ref.shape[0]

    @pl.when(c == 0)
    def _():
        _rmsnorm_rows(x_ref, g_ref, h_ref, HALO, tm)
        pos = i % tiles_per_seq
        hp = _rmsnorm_val(xp_ref[...], g_ref[...])
        hn = _rmsnorm_val(xn_ref[...], g_ref[...])
        h_ref[0:HALO, :] = jnp.where(pos == 0, 0.0, hp).astype(BF16)
        h_ref[HALO + tm:, :] = jnp.where(pos == tiles_per_seq - 1, 0.0, hn).astype(BF16)

    h = h_ref[...]
    zg_ref[...] = jnp.dot(h, wg_ref[...], preferred_element_type=F32)
    zu_ref[...] = jnp.dot(h, wu_ref[...], preferred_element_type=F32)

    def conv(z_ref, cw_ref, cb_ref):
        return (z_ref[pl.ds(HALO - 1, tm), :] * cw_ref[0:1, :]
                + z_ref[pl.ds(HALO, tm), :] * cw_ref[1:2, :]
                + z_ref[pl.ds(HALO + 1, tm), :] * cw_ref[2:3, :]
                + cb_ref[...])

    gate = conv(zg_ref, cwg_ref, cbg_ref)
    up = conv(zu_ref, cwu_ref, cbu_ref)
    act = (jax.nn.silu(gate) * up).astype(BF16)
    d = jnp.dot(act, wo_ref[...], preferred_element_type=F32)

    @pl.when(c == 0)
    def _():
        o_ref[...] = x_ref[...] + d

    @pl.when(c > 0)
    def _():
        o_ref[...] += d


def _ffn(x, gain, w_in, conv_w, conv_b, w_out, layer, seq_len, *, tm, tc):
    t, d = x.shape
    nch = D_FF // tc
    hb = tm // HALO
    nhb = t // HALO
    return pl.pallas_call(
        functools.partial(_ffn_kernel, tiles_per_seq=seq_len // tm),
        grid=(t // tm, nch),
        in_specs=[
            pl.BlockSpec((tm, d), lambda i, c: (i, 0)),
            pl.BlockSpec((HALO, d), lambda i, c: (jnp.maximum(i * hb - 1, 0), 0)),
            pl.BlockSpec((HALO, d), lambda i, c: (jnp.minimum((i + 1) * hb, nhb - 1), 0)),
            pl.BlockSpec((None, 1, d), lambda i, c: (layer, 0, 0)),
            pl.BlockSpec((None, d, tc), lambda i, c: (layer, 0, c)),
            pl.BlockSpec((None, d, tc), lambda i, c: (layer, 0, nch + c)),
            pl.BlockSpec((None, 3, tc), lambda i, c: (layer, 0, c)),
            pl.BlockSpec((None, 3, tc), lambda i, c: (layer, 0, nch + c)),
            pl.BlockSpec((None, 1, tc), lambda i, c: (layer, 0, c)),
            pl.BlockSpec((None, 1, tc), lambda i, c: (layer, 0, nch + c)),
            pl.BlockSpec((None, tc, d), lambda i, c: (layer, c, 0)),
        ],
        out_specs=pl.BlockSpec((tm, d), lambda i, c: (i, 0)),
        out_shape=jax.ShapeDtypeStruct((t, d), F32),
        scratch_shapes=[pltpu.VMEM((tm + 2 * HALO, d), BF16),
                        pltpu.VMEM((tm + 2 * HALO, tc), F32),
                        pltpu.VMEM((tm + 2 * HALO, tc), F32)],
        compiler_params=_params("parallel", "arbitrary"),
        name="conv_ffn",
    )(x, x, x, gain, w_in, w_in, conv_w, conv_w, conv_b, conv_b, w_out)


def _final_norm_kernel(x_ref, g_ref, o_ref):
    _rmsnorm_rows(x_ref, g_ref, o_ref, 0, x_ref.shape[0])


def _final_norm(x, gain, *, tm):
    t, d = x.shape
    return pl.pallas_call(
        _final_norm_kernel,
        grid=(t // tm,),
        in_specs=[pl.BlockSpec((tm, d), lambda i: (i, 0)),
                  pl.BlockSpec((1, d), lambda i: (0, 0))],
        out_specs=pl.BlockSpec((tm, d), lambda i: (i, 0)),
        out_shape=jax.ShapeDtypeStruct((t, d), F32),
        compiler_params=_params("parallel"),
        name="final_norm",
    )(x, gain)


def _trunk(x3, p):
    b, s, d = x3.shape
    x = x3.reshape(b * s, d)
    for i in range(DEPTH):
        j = i // 2
        if i % 2 == 0:
            z = _norm_proj(x, p["mix_norm"], i, p["a_w_in"], j, p["a_b_in"],
                           out_dtype=F32, tm=512, tn=1024)
            x = _gmlp_out(z, x, p["a_v_norm"], p["a_w_s"], p["a_b_s_t"], p["a_w_out"], j, tm=256)
        else:
            qkv = _norm_proj(x, p["mix_norm"], i, p["b_w_qkv"], j, None,
                             out_dtype=BF16, tm=512, tn=1024)
            o = _attention(qkv, p["bias"], p["b_sink"], j, s)
            x = _proj_residual(o, p["b_w_out"], j, x, tm=512)
        x = _ffn(x, p["ffn_norm"], p["f_w_in"], p["f_conv_w"], p["f_conv_b"], p["f_w_out"],
                 i, s, tm=512, tc=512)
    return _final_norm(x, p["final_norm"].reshape(1, d), tm=512).reshape(b, s, d)


def kernel(x_prompt, x_sample, rel_bias, mix_norm, ffn_norm, final_norm, a_w_in, a_b_in, a_v_norm, a_w_s, a_b_s, a_w_out, b_w_qkv, b_sink, b_w_out, f_w_in, f_conv_w, f_conv_b, f_w_out):
    row = lambda a: a[:, None, :]
    p = {
        "mix_norm": row(mix_norm), "ffn_norm": row(ffn_norm), "final_norm": final_norm,
        "a_w_in": a_w_in.astype(BF16), "a_b_in": row(a_b_in), "a_v_norm": row(a_v_norm),
        "a_w_s": a_w_s.astype(BF16), "a_b_s_t": jnp.swapaxes(a_b_s, 1, 2),
        "a_w_out": a_w_out.astype(BF16),
        "b_w_qkv": b_w_qkv.astype(BF16), "b_sink": b_sink, "b_w_out": b_w_out.astype(BF16),
        "f_w_in": f_w_in.astype(BF16), "f_conv_w": f_conv_w, "f_conv_b": row(f_conv_b),
        "f_w_out": f_w_out.astype(BF16),
        "bias": _bias_table(rel_bias),
    }
    return (_trunk(x_prompt, p), _trunk(x_sample, p))
```

```python
import functools

import numpy as np
import jax
import jax.numpy as jnp
from jax import lax
from jax.experimental import pallas as pl
from jax.experimental.pallas import tpu as pltpu

F32 = jnp.float32
BF16 = jnp.bfloat16

D_MODEL = 2048
DEPTH = 4
CHUNK = 128
A_GROUPS = 8
A_GROUP_DIM = D_MODEL // A_GROUPS
HEAD_DIM = 128
N_HEADS = D_MODEL // HEAD_DIM
N_KV_HEADS = 4
Q_PER_KV = N_HEADS // N_KV_HEADS
KV_DIM = N_KV_HEADS * HEAD_DIM
WINDOW = 128
BLOCK = 128
N_BUCKETS = 32
MAX_DISTANCE = 128
D_FF = 5632
EPS = 1e-6

VMEM_LIMIT_BYTES = 56 * 1024 * 1024
BF16_SUBLANES = 16
HALO = BF16_SUBLANES
NORM_ROWS = 64
FFN_PARTS = 2


def _params(*sem):
    return pltpu.CompilerParams(dimension_semantics=sem,
                                vmem_limit_bytes=VMEM_LIMIT_BYTES)


def _resident(block_shape, index_map):
    return pl.BlockSpec(block_shape, index_map, pipeline_mode=pl.Buffered(1))


def _rmsnorm_val(x, g):
    ms = jnp.mean(x * x, axis=-1, keepdims=True)
    return (x * lax.rsqrt(ms + EPS)) * g


def _rmsnorm_rows(x_ref, g_ref, dst_ref, dst_off, n_rows):
    g = g_ref[...]

    def body(c, carry):
        r = pl.multiple_of(c * NORM_ROWS, NORM_ROWS)
        x = x_ref[pl.ds(r, NORM_ROWS), :]
        dst_ref[pl.ds(dst_off + r, NORM_ROWS), :] = _rmsnorm_val(x, g).astype(dst_ref.dtype)
        return carry

    lax.fori_loop(0, n_rows // NORM_ROWS, body, 0)


def _norm_proj_body(x_ref, g_ref, w_ref, b_ref, o_ref, h_ref, tn):
    _rmsnorm_rows(x_ref, g_ref, h_ref, 0, x_ref.shape[0])
    for n in range(w_ref.shape[1] // tn):
        cols = slice(n * tn, (n + 1) * tn)
        acc = jnp.dot(h_ref[...], w_ref[:, cols], preferred_element_type=F32)
        if b_ref is not None:
            acc = jax.nn.gelu(acc + b_ref[:, cols])
        o_ref[:, cols] = acc.astype(o_ref.dtype)


def _norm_proj_kernel(x_ref, g_ref, w_ref, o_ref, h_ref, *, tn):
    _norm_proj_body(x_ref, g_ref, w_ref, None, o_ref, h_ref, tn)


def _norm_proj_gelu_kernel(x_ref, g_ref, w_ref, b_ref, o_ref, h_ref, *, tn):
    _norm_proj_body(x_ref, g_ref, w_ref, b_ref, o_ref, h_ref, tn)


def _norm_proj(x, gain, layer_g, w, layer_w, bias, *, out_dtype, tm, tn):
    t, d = x.shape
    n = w.shape[-1]
    in_specs = [
        pl.BlockSpec((tm, d), lambda i: (i, 0)),
        _resident((None, 1, d), lambda i: (layer_g, 0, 0)),
        _resident((None, d, n), lambda i: (layer_w, 0, 0)),
    ]
    args = [x, gain, w]
    if bias is not None:
        in_specs.append(_resident((None, 1, n), lambda i: (layer_w, 0, 0)))
        args.append(bias)
    return pl.pallas_call(
        functools.partial(_norm_proj_kernel if bias is None else _norm_proj_gelu_kernel, tn=tn),
        grid=(t // tm,),
        in_specs=in_specs,
        out_specs=pl.BlockSpec((tm, n), lambda i: (i, 0)),
        out_shape=jax.ShapeDtypeStruct((t, n), out_dtype),
        scratch_shapes=[pltpu.VMEM((tm, d), BF16)],
        compiler_params=_params("parallel"),
        name="norm_proj" if bias is None else "norm_proj_gelu",
    )(*args)


def _gmlp_out_kernel(u_ref, v_ref, vg_ref, ws_ref, bs_ref, wo_ref, x_ref, o_ref,
                     vn_ref, y_ref):
    tm = u_ref.shape[0]
    _rmsnorm_rows(v_ref, vg_ref, vn_ref, 0, tm)
    for n in range(tm // CHUNK):
        rows = slice(n * CHUNK, (n + 1) * CHUNK)
        for g in range(A_GROUPS):
            cols = slice(g * A_GROUP_DIM, (g + 1) * A_GROUP_DIM)
            s = jnp.dot(ws_ref[g], vn_ref[rows, cols], preferred_element_type=F32)
            s = s + bs_ref[:, g:g + 1]
            y_ref[rows, cols] = (u_ref[rows, cols] * s).astype(BF16)
    o_ref[...] = x_ref[...] + jnp.dot(y_ref[...], wo_ref[...], preferred_element_type=F32)


def _gmlp_out(z, x, v_norm, w_s, b_s_t, w_out, layer, *, tm):
    t, d = x.shape
    return pl.pallas_call(
        _gmlp_out_kernel,
        grid=(t // tm,),
        in_specs=[
            pl.BlockSpec((tm, d), lambda i: (i, 0)),
            pl.BlockSpec((tm, d), lambda i: (i, 1)),
            _resident((None, 1, d), lambda i: (layer, 0, 0)),
            _resident((None, A_GROUPS, CHUNK, CHUNK), lambda i: (layer, 0, 0, 0)),
            _resident((None, CHUNK, A_GROUPS), lambda i: (layer, 0, 0)),
            _resident((None, d, d), lambda i: (layer, 0, 0)),
            pl.BlockSpec((tm, d), lambda i: (i, 0)),
        ],
        out_specs=pl.BlockSpec((tm, d), lambda i: (i, 0)),
        out_shape=jax.ShapeDtypeStruct((t, d), F32),
        scratch_shapes=[pltpu.VMEM((tm, d), BF16), pltpu.VMEM((tm, d), BF16)],
        compiler_params=_params("parallel"),
        name="gmlp_out",
    )(z, z, v_norm, w_s, b_s_t, w_out, x)


def _relative_bucket(rel):
    half = N_BUCKETS // 2
    max_exact = half // 2
    ret = (rel > 0).astype(np.int32) * half
    n = np.abs(rel)
    nf = np.maximum(n, 1).astype(np.float32)
    large = max_exact + (np.log(nf / max_exact) / np.log(MAX_DISTANCE / max_exact)
                         * (half - max_exact)).astype(np.int32)
    large = np.minimum(large, half - 1)
    return (ret + np.where(n < max_exact, n, large)).astype(np.int32)


def _band_bucket():
    a = np.arange(BLOCK)[:, None]
    c = np.arange(3 * BLOCK)[None, :]
    rel = c - BLOCK - a
    return np.where(np.abs(rel) <= WINDOW, _relative_bucket(rel), -1).astype(np.int32)


def _bias_table_kernel(bucket_ref, rb_ref, o_ref):
    bucket = bucket_ref[...]
    for h in range(N_HEADS):
        acc = jnp.full(bucket.shape, -jnp.inf, F32)
        for b in range(N_BUCKETS):
            acc = jnp.where(bucket == b, rb_ref[b, h], acc)
        o_ref[h] = acc


def _bias_table(rel_bias):
    return pl.pallas_call(
        _bias_table_kernel,
        in_specs=[pl.BlockSpec(memory_space=pltpu.VMEM),
                  pl.BlockSpec(memory_space=pltpu.SMEM)],
        out_specs=pl.BlockSpec(memory_space=pltpu.VMEM),
        out_shape=jax.ShapeDtypeStruct((N_HEADS, BLOCK, 3 * BLOCK), F32),
        name="bias_table",
    )(jnp.asarray(_band_bucket()), rel_bias)


def _attn_kernel(q_ref, kp_ref, kc_ref, kn_ref, vp_ref, vc_ref, vn_ref, bias_ref, sink_ref,
                 o_ref, *, blocks_per_seq, layer):
    blk = pl.program_id(0) % blocks_per_seq
    col = lax.broadcasted_iota(jnp.int32, (1, 3 * BLOCK), 1)
    first_col = jnp.where(blk == 0, BLOCK, 0)
    end_col = jnp.where(blk == blocks_per_seq - 1, 2 * BLOCK, 3 * BLOCK)
    edge = (col < first_col) | (col >= end_col)
    scale = HEAD_DIM ** -0.5

    def head_cols(h):
        return slice(h * HEAD_DIM, (h + 1) * HEAD_DIM)

    def qk(kh):
        kc = head_cols(kh)
        k = jnp.concatenate([kp_ref[:, kc], kc_ref[:, kc], kn_ref[:, kc]], axis=0)
        q = jnp.concatenate([q_ref[:, head_cols(kh * Q_PER_KV + g)] for g in range(Q_PER_KV)],
                            axis=0)
        return lax.dot_general(q, k, (((1,), (1,)), ((), ())), preferred_element_type=F32)

    def softmax_pv(kh, logits):
        probs = []
        for g in range(Q_PER_KV):
            h = kh * Q_PER_KV + g
            lg = logits[g * BLOCK:(g + 1) * BLOCK] * scale + bias_ref[h]
            lg = jnp.where(edge, -jnp.inf, lg)
            sink = sink_ref[layer, h]
            m = jnp.maximum(jnp.max(lg, axis=-1, keepdims=True), sink)
            p = jnp.exp(lg - m)
            denom = jnp.sum(p, axis=-1, keepdims=True) + jnp.exp(sink - m)
            probs.append((p / denom).astype(BF16))
        kc = head_cols(kh)
        v = jnp.concatenate([vp_ref[:, kc], vc_ref[:, kc], vn_ref[:, kc]], axis=0)
        o = jnp.dot(jnp.concatenate(probs, axis=0), v, preferred_element_type=F32)
        for g in range(Q_PER_KV):
            o_ref[:, head_cols(kh * Q_PER_KV + g)] = o[g * BLOCK:(g + 1) * BLOCK].astype(o_ref.dtype)

    logits = qk(0)
    for kh in range(N_KV_HEADS):
        nxt = qk(kh + 1) if kh + 1 < N_KV_HEADS else None
        softmax_pv(kh, logits)
        logits = nxt


def _attention(qkv, bias, sink, layer, seq_len):
    t = qkv.shape[0]
    nblk = t // BLOCK
    bps = seq_len // BLOCK
    k0 = D_MODEL // KV_DIM
    v0 = k0 + 1

    def kv_spec(col, shift):
        return pl.BlockSpec((BLOCK, KV_DIM),
                            lambda i: (jnp.clip(i + shift, 0, nblk - 1), col))

    return pl.pallas_call(
        functools.partial(_attn_kernel, blocks_per_seq=bps, layer=layer),
        grid=(nblk,),
        in_specs=[
            pl.BlockSpec((BLOCK, D_MODEL), lambda i: (i, 0)),
            kv_spec(k0, -1), kv_spec(k0, 0), kv_spec(k0, 1),
            kv_spec(v0, -1), kv_spec(v0, 0), kv_spec(v0, 1),
            _resident((N_HEADS, BLOCK, 3 * BLOCK), lambda i: (0, 0, 0)),
            pl.BlockSpec(memory_space=pltpu.SMEM),
        ],
        out_specs=pl.BlockSpec((BLOCK, D_MODEL), lambda i: (i, 0)),
        out_shape=jax.ShapeDtypeStruct((t, D_MODEL), BF16),
        compiler_params=_params("parallel"),
        name="band_attention",
    )(qkv, qkv, qkv, qkv, qkv, qkv, qkv, bias, sink)


def _proj_residual_kernel(a_ref, w_ref, x_ref, o_ref):
    o_ref[...] = x_ref[...] + jnp.dot(a_ref[...], w_ref[...], preferred_element_type=F32)


def _proj_residual(a, w, layer, x, *, tm):
    t, d = x.shape
    k = a.shape[1]
    return pl.pallas_call(
        _proj_residual_kernel,
        grid=(t // tm,),
        in_specs=[
            pl.BlockSpec((tm, k), lambda i: (i, 0)),
            _resident((None, k, d), lambda i: (layer, 0, 0)),
            pl.BlockSpec((tm, d), lambda i: (i, 0)),
        ],
        out_specs=pl.BlockSpec((tm, d), lambda i: (i, 0)),
        out_shape=jax.ShapeDtypeStruct((t, d), F32),
        compiler_params=_params("parallel"),
        name="proj_residual",
    )(a, w, x)


def _ffn_kernel(x_ref, xp_ref, xn_ref, g_ref, wg_ref, wu_ref, cwg_ref, cwu_ref, cbg_ref,
                cbu_ref, wo_ref, o_ref, h_ref, *z_refs, tiles_per_seq):
    i = pl.program_id(0)
    c = pl.program_id(1)
    tm = x_ref.shape[0]
    tc = wg_ref.shape[1]
    n_parts = len(z_refs) // 2
    tp = tc // n_parts

    @pl.when(c == 0)
    def _():
        _rmsnorm_rows(x_ref, g_ref, h_ref, HALO, tm)
        pos = i % tiles_per_seq
        hp = _rmsnorm_val(xp_ref[...], g_ref[...])
        hn = _rmsnorm_val(xn_ref[...], g_ref[...])
        h_ref[0:HALO, :] = jnp.where(pos == 0, 0.0, hp).astype(BF16)
        h_ref[HALO + tm:, :] = jnp.where(pos == tiles_per_seq - 1, 0.0, hn).astype(BF16)
        o_ref[...] = x_ref[...]

    h = h_ref[...]
    for p in range(n_parts):
        cols = slice(p * tp, (p + 1) * tp)
        z_refs[2 * p][...] = jnp.dot(h, wg_ref[:, cols], preferred_element_type=F32)
        z_refs[2 * p + 1][...] = jnp.dot(h, wu_ref[:, cols], preferred_element_type=F32)

    def conv(z_ref, cw_ref, cb_ref, cols):
        return (z_ref[pl.ds(HALO - 1, tm), :] * cw_ref[0:1, cols]
                + z_ref[pl.ds(HALO, tm), :] * cw_ref[1:2, cols]
                + z_ref[pl.ds(HALO + 1, tm), :] * cw_ref[2:3, cols]
                + cb_ref[:, cols])

    d = None
    for p in range(n_parts):
        cols = slice(p * tp, (p + 1) * tp)
        gate = conv(z_refs[2 * p], cwg_ref, cbg_ref, cols)
        up = conv(z_refs[2 * p + 1], cwu_ref, cbu_ref, cols)
        act = (jax.nn.silu(gate) * up).astype(BF16)
        dp = jnp.dot(act, wo_ref[cols, :], preferred_element_type=F32)
        d = dp if d is None else d + dp
    o_ref[...] += d


def _ffn(x, gain, w_in, conv_w, conv_b, w_out, layer, seq_len, *, tm, tc):
    t, d = x.shape
    nch = D_FF // tc
    hb = tm // HALO
    nhb = t // HALO
    return pl.pallas_call(
        functools.partial(_ffn_kernel, tiles_per_seq=seq_len // tm),
        grid=(t // tm, nch),
        in_specs=[
            pl.BlockSpec((tm, d), lambda i, c: (i, 0)),
            pl.BlockSpec((HALO, d), lambda i, c: (jnp.maximum(i * hb - 1, 0), 0)),
            pl.BlockSpec((HALO, d), lambda i, c: (jnp.minimum((i + 1) * hb, nhb - 1), 0)),
            _resident((None, 1, d), lambda i, c: (layer, 0, 0)),
            pl.BlockSpec((None, d, tc), lambda i, c: (layer, 0, c)),
            pl.BlockSpec((None, d, tc), lambda i, c: (layer, 0, nch + c)),
            pl.BlockSpec((None, 3, tc), lambda i, c: (layer, 0, c)),
            pl.BlockSpec((None, 3, tc), lambda i, c: (layer, 0, nch + c)),
            pl.BlockSpec((None, 1, tc), lambda i, c: (layer, 0, c)),
            pl.BlockSpec((None, 1, tc), lambda i, c: (layer, 0, nch + c)),
            pl.BlockSpec((None, tc, d), lambda i, c: (layer, c, 0)),
        ],
        out_specs=pl.BlockSpec((tm, d), lambda i, c: (i, 0)),
        out_shape=jax.ShapeDtypeStruct((t, d), F32),
        scratch_shapes=[pltpu.VMEM((tm + 2 * HALO, d), BF16)]
        + [pltpu.VMEM((tm + 2 * HALO, tc // FFN_PARTS), F32)] * (2 * FFN_PARTS),
        compiler_params=_params("parallel", "arbitrary"),
        name="conv_ffn",
    )(x, x, x, gain, w_in, w_in, conv_w, conv_w, conv_b, conv_b, w_out)


def _final_norm_kernel(x_ref, g_ref, o_ref):
    _rmsnorm_rows(x_ref, g_ref, o_ref, 0, x_ref.shape[0])


def _final_norm(x, gain, *, tm):
    t, d = x.shape
    return pl.pallas_call(
        _final_norm_kernel,
        grid=(t // tm,),
        in_specs=[pl.BlockSpec((tm, d), lambda i: (i, 0)),
                  _resident((1, d), lambda i: (0, 0))],
        out_specs=pl.BlockSpec((tm, d), lambda i: (i, 0)),
        out_shape=jax.ShapeDtypeStruct((t, d), F32),
        compiler_params=_params("parallel"),
        name="final_norm",
    )(x, gain)


def _trunk(x3, p):
    b, s, d = x3.shape
    x = x3.reshape(b * s, d)
    for i in range(DEPTH):
        j = i // 2
        if i % 2 == 0:
            z = _norm_proj(x, p["mix_norm"], i, p["a_w_in"], j, p["a_b_in"],
                           out_dtype=F32, tm=512, tn=1024)
            x = _gmlp_out(z, x, p["a_v_norm"], p["a_w_s"], p["a_b_s_t"], p["a_w_out"], j, tm=256)
        else:
            qkv = _norm_proj(x, p["mix_norm"], i, p["b_w_qkv"], j, None,
                             out_dtype=BF16, tm=512, tn=1024)
            o = _attention(qkv, p["bias"], p["b_sink"], j, s)
            x = _proj_residual(o, p["b_w_out"], j, x, tm=512)
        x = _ffn(x, p["ffn_norm"], p["f_w_in"], p["f_conv_w"], p["f_conv_b"], p["f_w_out"],
                 i, s, tm=512, tc=512)
    return _final_norm(x, p["final_norm"].reshape(1, d), tm=512).reshape(b, s, d)


def kernel(x_prompt, x_sample, rel_bias, mix_norm, ffn_norm, final_norm, a_w_in, a_b_in, a_v_norm, a_w_s, a_b_s, a_w_out, b_w_qkv, b_sink, b_w_out, f_w_in, f_conv_w, f_conv_b, f_w_out):
    row = lambda a: a[:, None, :]
    p = {
        "mix_norm": row(mix_norm), "ffn_norm": row(ffn_norm), "final_norm": final_norm,
        "a_w_in": a_w_in.astype(BF16), "a_b_in": row(a_b_in), "a_v_norm": row(a_v_norm),
        "a_w_s": a_w_s.astype(BF16), "a_b_s_t": jnp.swapaxes(a_b_s, 1, 2),
        "a_w_out": a_w_out.astype(BF16),
        "b_w_qkv": b_w_qkv.astype(BF16), "b_sink": b_sink, "b_w_out": b_w_out.astype(BF16),
        "f_w_in": f_w_in.astype(BF16), "f_conv_w": f_conv_w, "f_conv_b": row(f_conv_b),
        "f_w_out": f_w_out.astype(BF16),
        "bias": _bias_table(rel_bias),
    }
    return (_trunk(x_prompt, p), _trunk(x_sample, p))
```

```python
import functools

import numpy as np
import jax
import jax.numpy as jnp
from jax import lax
from jax.experimental import pallas as pl
from jax.experimental.pallas import tpu as pltpu

F32 = jnp.float32
BF16 = jnp.bfloat16

D_MODEL = 2048
DEPTH = 4
CHUNK = 128
A_GROUPS = 8
A_GROUP_DIM = D_MODEL // A_GROUPS
HEAD_DIM = 128
N_HEADS = D_MODEL // HEAD_DIM
N_KV_HEADS = 4
Q_PER_KV = N_HEADS // N_KV_HEADS
KV_DIM = N_KV_HEADS * HEAD_DIM
WINDOW = 128
BLOCK = 128
N_BUCKETS = 32
MAX_DISTANCE = 128
D_FF = 5632
EPS = 1e-6

VMEM_LIMIT_BYTES = 60 * 1024 * 1024
BF16_SUBLANES = 16
HALO = BF16_SUBLANES
NORM_ROWS = 64
WEIGHT_COLS = 512
FFN_PARTS = 2


def _params(*sem):
    return pltpu.CompilerParams(dimension_semantics=sem,
                                vmem_limit_bytes=VMEM_LIMIT_BYTES)


def _resident(block_shape, index_map):
    return pl.BlockSpec(block_shape, index_map, pipeline_mode=pl.Buffered(1))


def _rmsnorm_val(x, g):
    ms = jnp.mean(x * x, axis=-1, keepdims=True)
    return (x * lax.rsqrt(ms + EPS)) * g


def _rmsnorm_rows(x_ref, g_ref, dst_ref, dst_off, n_rows):
    g = g_ref[...]

    def body(c, carry):
        r = pl.multiple_of(c * NORM_ROWS, NORM_ROWS)
        x = x_ref[pl.ds(r, NORM_ROWS), :]
        dst_ref[pl.ds(dst_off + r, NORM_ROWS), :] = _rmsnorm_val(x, g).astype(dst_ref.dtype)
        return carry

    lax.fori_loop(0, n_rows // NORM_ROWS, body, 0, unroll=2)


def _norm_proj_kernel(x_ref, g_ref, w_ref, o_ref, h_ref):
    _rmsnorm_rows(x_ref, g_ref, h_ref, 0, x_ref.shape[0])
    for n in range(w_ref.shape[0]):
        cols = slice(n * WEIGHT_COLS, (n + 1) * WEIGHT_COLS)
        o_ref[:, cols] = jnp.dot(h_ref[...], w_ref[n],
                                 preferred_element_type=F32).astype(o_ref.dtype)


def _norm_proj_gelu_kernel(x_ref, g_ref, w_ref, b_ref, o_ref, h_ref):
    _rmsnorm_rows(x_ref, g_ref, h_ref, 0, x_ref.shape[0])
    for n in range(w_ref.shape[0]):
        cols = slice(n * WEIGHT_COLS, (n + 1) * WEIGHT_COLS)
        acc = jnp.dot(h_ref[...], w_ref[n], preferred_element_type=F32)
        o_ref[:, cols] = jax.nn.gelu(acc + b_ref[:, cols]).astype(o_ref.dtype)


def _column_blocks(w):
    l, k, n = w.shape
    return w.astype(BF16).reshape(l, k, n // WEIGHT_COLS, WEIGHT_COLS).transpose(0, 2, 1, 3)


def _proj_residual_blocks(a_ref, w_ref, x_ref, o_ref):
    for n in range(w_ref.shape[0]):
        cols = slice(n * WEIGHT_COLS, (n + 1) * WEIGHT_COLS)
        o_ref[:, cols] = x_ref[:, cols] + jnp.dot(a_ref[...], w_ref[n],
                                                  preferred_element_type=F32)


def _norm_proj(x, gain, layer_g, w, layer_w, bias, *, out_dtype, tm):
    t, d = x.shape
    nb = w.shape[1]
    n = nb * WEIGHT_COLS
    in_specs = [
        pl.BlockSpec((tm, d), lambda i: (i, 0)),
        _resident((None, 1, d), lambda i: (layer_g, 0, 0)),
        _resident((None, nb, d, WEIGHT_COLS), lambda i: (layer_w, 0, 0, 0)),
    ]
    args = [x, gain, w]
    if bias is not None:
        in_specs.append(_resident((None, 1, n), lambda i: (layer_w, 0, 0)))
        args.append(bias)
    return pl.pallas_call(
        _norm_proj_kernel if bias is None else _norm_proj_gelu_kernel,
        grid=(t // tm,),
        in_specs=in_specs,
        out_specs=pl.BlockSpec((tm, n), lambda i: (i, 0)),
        out_shape=jax.ShapeDtypeStruct((t, n), out_dtype),
        scratch_shapes=[pltpu.VMEM((tm, d), BF16)],
        compiler_params=_params("parallel"),
        name="norm_proj" if bias is None else "norm_proj_gelu",
    )(*args)


def _gmlp_out_kernel(u_ref, v_ref, vg_ref, ws_ref, bs_ref, wo_ref, x_ref, o_ref,
                     vn_ref, y_ref):
    tm = u_ref.shape[0]
    _rmsnorm_rows(v_ref, vg_ref, vn_ref, 0, tm)
    for n in range(tm // CHUNK):
        rows = slice(n * CHUNK, (n + 1) * CHUNK)
        for g in range(A_GROUPS):
            cols = slice(g * A_GROUP_DIM, (g + 1) * A_GROUP_DIM)
            s = jnp.dot(ws_ref[g], vn_ref[rows, cols], preferred_element_type=F32)
            s = s + bs_ref[:, g:g + 1]
            y_ref[rows, cols] = (u_ref[rows, cols] * s).astype(BF16)
    _proj_residual_blocks(y_ref, wo_ref, x_ref, o_ref)


def _gmlp_out(z, x, v_norm, w_s, b_s_t, w_out, layer, *, tm):
    t, d = x.shape
    return pl.pallas_call(
        _gmlp_out_kernel,
        grid=(t // tm,),
        in_specs=[
            pl.BlockSpec((tm, d), lambda i: (i, 0)),
            pl.BlockSpec((tm, d), lambda i: (i, 1)),
            _resident((None, 1, d), lambda i: (layer, 0, 0)),
            _resident((None, A_GROUPS, CHUNK, CHUNK), lambda i: (layer, 0, 0, 0)),
            _resident((None, CHUNK, A_GROUPS), lambda i: (layer, 0, 0)),
            _resident((None, d // WEIGHT_COLS, d, WEIGHT_COLS), lambda i: (layer, 0, 0, 0)),
            pl.BlockSpec((tm, d), lambda i: (i, 0)),
        ],
        out_specs=pl.BlockSpec((tm, d), lambda i: (i, 0)),
        out_shape=jax.ShapeDtypeStruct((t, d), F32),
        scratch_shapes=[pltpu.VMEM((tm, d), BF16), pltpu.VMEM((tm, d), BF16)],
        compiler_params=_params("parallel"),
        name="gmlp_out",
    )(z, z, v_norm, w_s, b_s_t, w_out, x)


def _relative_bucket(rel):
    half = N_BUCKETS // 2
    max_exact = half // 2
    ret = (rel > 0).astype(np.int32) * half
    n = np.abs(rel)
    nf = np.maximum(n, 1).astype(np.float32)
    large = max_exact + (np.log(nf / max_exact) / np.log(MAX_DISTANCE / max_exact)
                         * (half - max_exact)).astype(np.int32)
    large = np.minimum(large, half - 1)
    return (ret + np.where(n < max_exact, n, large)).astype(np.int32)


def _band_bucket():
    a = np.arange(BLOCK)[:, None]
    c = np.arange(3 * BLOCK)[None, :]
    rel = c - BLOCK - a
    return np.where(np.abs(rel) <= WINDOW, _relative_bucket(rel), -1).astype(np.int32)


def _bias_table_kernel(bucket_ref, rb_ref, o_ref):
    bucket = bucket_ref[...]
    for h in range(N_HEADS):
        acc = jnp.full(bucket.shape, -jnp.inf, F32)
        for b in range(N_BUCKETS):
            acc = jnp.where(bucket == b, rb_ref[b, h], acc)
        o_ref[h] = acc


def _bias_table(rel_bias):
    return pl.pallas_call(
        _bias_table_kernel,
        in_specs=[pl.BlockSpec(memory_space=pltpu.VMEM),
                  pl.BlockSpec(memory_space=pltpu.SMEM)],
        out_specs=pl.BlockSpec(memory_space=pltpu.VMEM),
        out_shape=jax.ShapeDtypeStruct((N_HEADS, BLOCK, 3 * BLOCK), F32),
        name="bias_table",
    )(jnp.asarray(_band_bucket()), rel_bias)


def _attn_kernel(q_ref, kp_ref, kc_ref, kn_ref, vp_ref, vc_ref, vn_ref, bias_ref, sink_ref,
                 o_ref, *, blocks_per_seq, layer):
    blk = pl.program_id(0) % blocks_per_seq
    col = lax.broadcasted_iota(jnp.int32, (1, 3 * BLOCK), 1)
    first_col = jnp.where(blk == 0, BLOCK, 0)
    end_col = jnp.where(blk == blocks_per_seq - 1, 2 * BLOCK, 3 * BLOCK)
    edge = (col < first_col) | (col >= end_col)
    scale = HEAD_DIM ** -0.5

    def head_cols(h):
        return slice(h * HEAD_DIM, (h + 1) * HEAD_DIM)

    def qk(kh):
        kc = head_cols(kh)
        k = jnp.concatenate([kp_ref[:, kc], kc_ref[:, kc], kn_ref[:, kc]], axis=0)
        q = jnp.concatenate([q_ref[:, head_cols(kh * Q_PER_KV + g)] for g in range(Q_PER_KV)],
                            axis=0)
        return lax.dot_general(q, k, (((1,), (1,)), ((), ())), preferred_element_type=F32)

    def softmax_pv(kh, logits):
        probs = []
        for g in range(Q_PER_KV):
            h = kh * Q_PER_KV + g
            lg = logits[g * BLOCK:(g + 1) * BLOCK] * scale + bias_ref[h]
            lg = jnp.where(edge, -jnp.inf, lg)
            sink = sink_ref[layer, h]
            m = jnp.maximum(jnp.max(lg, axis=-1, keepdims=True), sink)
            p = jnp.exp(lg - m)
            denom = jnp.sum(p, axis=-1, keepdims=True) + jnp.exp(sink - m)
            probs.append((p / denom).astype(BF16))
        kc = head_cols(kh)
        v = jnp.concatenate([vp_ref[:, kc], vc_ref[:, kc], vn_ref[:, kc]], axis=0)
        o = jnp.dot(jnp.concatenate(probs, axis=0), v, preferred_element_type=F32)
        for g in range(Q_PER_KV):
            o_ref[:, head_cols(kh * Q_PER_KV + g)] = o[g * BLOCK:(g + 1) * BLOCK].astype(o_ref.dtype)

    logits = qk(0)
    for kh in range(N_KV_HEADS):
        nxt = qk(kh + 1) if kh + 1 < N_KV_HEADS else None
        softmax_pv(kh, logits)
        logits = nxt


def _attention(qkv, bias, sink, layer, seq_len):
    t = qkv.shape[0]
    nblk = t // BLOCK
    bps = seq_len // BLOCK
    k0 = D_MODEL // KV_DIM
    v0 = k0 + 1

    def kv_spec(col, shift):
        return pl.BlockSpec((BLOCK, KV_DIM),
                            lambda i: (jnp.clip(i + shift, 0, nblk - 1), col))

    return pl.pallas_call(
        functools.partial(_attn_kernel, blocks_per_seq=bps, layer=layer),
        grid=(nblk,),
        in_specs=[
            pl.BlockSpec((BLOCK, D_MODEL), lambda i: (i, 0)),
            kv_spec(k0, -1), kv_spec(k0, 0), kv_spec(k0, 1),
            kv_spec(v0, -1), kv_spec(v0, 0), kv_spec(v0, 1),
            _resident((N_HEADS, BLOCK, 3 * BLOCK), lambda i: (0, 0, 0)),
            pl.BlockSpec(memory_space=pltpu.SMEM),
        ],
        out_specs=pl.BlockSpec((BLOCK, D_MODEL), lambda i: (i, 0)),
        out_shape=jax.ShapeDtypeStruct((t, D_MODEL), BF16),
        compiler_params=_params("parallel"),
        name="band_attention",
    )(qkv, qkv, qkv, qkv, qkv, qkv, qkv, bias, sink)


def _proj_residual_kernel(a_ref, w_ref, x_ref, o_ref):
    _proj_residual_blocks(a_ref, w_ref, x_ref, o_ref)


def _proj_residual(a, w, layer, x, *, tm):
    t, d = x.shape
    k = a.shape[1]
    return pl.pallas_call(
        _proj_residual_kernel,
        grid=(t // tm,),
        in_specs=[
            pl.BlockSpec((tm, k), lambda i: (i, 0)),
            _resident((None, d // WEIGHT_COLS, k, WEIGHT_COLS), lambda i: (layer, 0, 0, 0)),
            pl.BlockSpec((tm, d), lambda i: (i, 0)),
        ],
        out_specs=pl.BlockSpec((tm, d), lambda i: (i, 0)),
        out_shape=jax.ShapeDtypeStruct((t, d), F32),
        compiler_params=_params("parallel"),
        name="proj_residual",
    )(a, w, x)


def _ffn_kernel(x_ref, xp_ref, xn_ref, g_ref, wg_ref, wu_ref, cwg_ref, cwu_ref, cbg_ref,
                cbu_ref, wo_ref, o_ref, h_ref, *z_refs, tiles_per_seq):
    i = pl.program_id(0)
    c = pl.program_id(1)
    tm = x_ref.shape[0]
    tc = wg_ref.shape[1]
    n_parts = len(z_refs) // 2
    tp = tc // n_parts

    @pl.when(c == 0)
    def _():
        _rmsnorm_rows(x_ref, g_ref, h_ref, HALO, tm)
        pos = i % tiles_per_seq
        hp = _rmsnorm_val(xp_ref[...], g_ref[...])
        hn = _rmsnorm_val(xn_ref[...], g_ref[...])
        h_ref[0:HALO, :] = jnp.where(pos == 0, 0.0, hp).astype(BF16)
        h_ref[HALO + tm:, :] = jnp.where(pos == tiles_per_seq - 1, 0.0, hn).astype(BF16)
        o_ref[...] = x_ref[...]

    h = h_ref[...]
    part_cols = [slice(p * tp, (p + 1) * tp) for p in range(n_parts)]
    for p, cols in enumerate(part_cols):
        z_refs[2 * p][...] = jnp.dot(h, wg_ref[:, cols], preferred_element_type=F32)
    for p, cols in enumerate(part_cols):
        z_refs[2 * p + 1][...] = jnp.dot(h, wu_ref[:, cols], preferred_element_type=F32)

    def conv(z_ref, cw_ref, cb_ref, cols):
        return (z_ref[pl.ds(HALO - 1, tm), :] * cw_ref[0:1, cols]
                + z_ref[pl.ds(HALO, tm), :] * cw_ref[1:2, cols]
                + z_ref[pl.ds(HALO + 1, tm), :] * cw_ref[2:3, cols]
                + cb_ref[:, cols])

    gates = [jax.nn.silu(conv(z_refs[2 * p], cwg_ref, cbg_ref, cols))
             for p, cols in enumerate(part_cols)]
    for p, cols in enumerate(part_cols):
        up = conv(z_refs[2 * p + 1], cwu_ref, cbu_ref, cols)
        act = (gates[p] * up).astype(BF16)
        o_ref[...] += jnp.dot(act, wo_ref[cols, :], preferred_element_type=F32)


def _ffn(x, gain, w_in, conv_w, conv_b, w_out, layer, seq_len, *, tm, tc):
    t, d = x.shape
    nch = D_FF // tc
    hb = tm // HALO
    nhb = t // HALO
    return pl.pallas_call(
        functools.partial(_ffn_kernel, tiles_per_seq=seq_len // tm),
        grid=(t // tm, nch),
        in_specs=[
            pl.BlockSpec((tm, d), lambda i, c: (i, 0)),
            pl.BlockSpec((HALO, d), lambda i, c: (jnp.maximum(i * hb - 1, 0), 0)),
            pl.BlockSpec((HALO, d), lambda i, c: (jnp.minimum((i + 1) * hb, nhb - 1), 0)),
            _resident((None, 1, d), lambda i, c: (layer, 0, 0)),
            pl.BlockSpec((None, d, tc), lambda i, c: (layer, 0, c)),
            pl.BlockSpec((None, d, tc), lambda i, c: (layer, 0, nch + c)),
            pl.BlockSpec((None, 3, tc), lambda i, c: (layer, 0, c)),
            pl.BlockSpec((None, 3, tc), lambda i, c: (layer, 0, nch + c)),
            pl.BlockSpec((None, 1, tc), lambda i, c: (layer, 0, c)),
            pl.BlockSpec((None, 1, tc), lambda i, c: (layer, 0, nch + c)),
            pl.BlockSpec((None, tc, d), lambda i, c: (layer, c, 0)),
        ],
        out_specs=pl.BlockSpec((tm, d), lambda i, c: (i, 0)),
        out_shape=jax.ShapeDtypeStruct((t, d), F32),
        scratch_shapes=[pltpu.VMEM((tm + 2 * HALO, d), BF16)]
        + [pltpu.VMEM((tm + 2 * HALO, tc // FFN_PARTS), F32)] * (2 * FFN_PARTS),
        compiler_params=_params("parallel", "arbitrary"),
        name="conv_ffn",
    )(x, x, x, gain, w_in, w_in, conv_w, conv_w, conv_b, conv_b, w_out)


def _final_norm_kernel(x_ref, g_ref, o_ref):
    _rmsnorm_rows(x_ref, g_ref, o_ref, 0, x_ref.shape[0])


def _final_norm(x, gain, *, tm):
    t, d = x.shape
    return pl.pallas_call(
        _final_norm_kernel,
        grid=(t // tm,),
        in_specs=[pl.BlockSpec((tm, d), lambda i: (i, 0)),
                  _resident((1, d), lambda i: (0, 0))],
        out_specs=pl.BlockSpec((tm, d), lambda i: (i, 0)),
        out_shape=jax.ShapeDtypeStruct((t, d), F32),
        compiler_params=_params("parallel"),
        name="final_norm",
    )(x, gain)


def _trunk(x3, p):
    b, s, d = x3.shape
    x = x3.reshape(b * s, d)
    for i in range(DEPTH):
        j = i // 2
        if i % 2 == 0:
            z = _norm_proj(x, p["mix_norm"], i, p["a_w_in"], j, p["a_b_in"],
                           out_dtype=F32, tm=512)
            x = _gmlp_out(z, x, p["a_v_norm"], p["a_w_s"], p["a_b_s_t"], p["a_w_out"], j, tm=512)
        else:
            qkv = _norm_proj(x, p["mix_norm"], i, p["b_w_qkv"], j, None,
                             out_dtype=BF16, tm=512)
            o = _attention(qkv, p["bias"], p["b_sink"], j, s)
            x = _proj_residual(o, p["b_w_out"], j, x, tm=512)
        x = _ffn(x, p["ffn_norm"], p["f_w_in"], p["f_conv_w"], p["f_conv_b"], p["f_w_out"],
                 i, s, tm=1024, tc=512)
    return _final_norm(x, p["final_norm"].reshape(1, d), tm=512).reshape(b, s, d)


def kernel(x_prompt, x_sample, rel_bias, mix_norm, ffn_norm, final_norm, a_w_in, a_b_in, a_v_norm, a_w_s, a_b_s, a_w_out, b_w_qkv, b_sink, b_w_out, f_w_in, f_conv_w, f_conv_b, f_w_out):
    row = lambda a: a[:, None, :]
    p = {
        "mix_norm": row(mix_norm), "ffn_norm": row(ffn_norm), "final_norm": final_norm,
        "a_w_in": _column_blocks(a_w_in), "a_b_in": row(a_b_in), "a_v_norm": row(a_v_norm),
        "a_w_s": a_w_s.astype(BF16), "a_b_s_t": jnp.swapaxes(a_b_s, 1, 2),
        "a_w_out": _column_blocks(a_w_out),
        "b_w_qkv": _column_blocks(b_w_qkv), "b_sink": b_sink, "b_w_out": _column_blocks(b_w_out),
        "f_w_in": f_w_in.astype(BF16), "f_conv_w": f_conv_w, "f_conv_b": row(f_conv_b),
        "f_w_out": f_w_out.astype(BF16),
        "bias": _bias_table(rel_bias),
    }
    return (_trunk(x_prompt, p), _trunk(x_sample, p))
```

```python
import functools

import numpy as np
import jax
import jax.numpy as jnp
from jax import lax
from jax.experimental import pallas as pl
from jax.experimental.pallas import tpu as pltpu

F32 = jnp.float32
BF16 = jnp.bfloat16

D_MODEL = 2048
DEPTH = 4
CHUNK = 128
A_GROUPS = 8
A_GROUP_DIM = D_MODEL // A_GROUPS
HEAD_DIM = 128
N_HEADS = D_MODEL // HEAD_DIM
N_KV_HEADS = 4
Q_PER_KV = N_HEADS // N_KV_HEADS
KV_DIM = N_KV_HEADS * HEAD_DIM
WINDOW = 128
BLOCK = 128
N_BUCKETS = 32
MAX_DISTANCE = 128
D_FF = 5632
EPS = 1e-6

VMEM_LIMIT_BYTES = 60 * 1024 * 1024
BF16_SUBLANES = 16
HALO = BF16_SUBLANES
NORM_ROWS = 64
WEIGHT_COLS = 512
FFN_PARTS = 2


def _params(*sem):
    return pltpu.CompilerParams(dimension_semantics=sem,
                                vmem_limit_bytes=VMEM_LIMIT_BYTES)


def _resident(block_shape, index_map):
    return pl.BlockSpec(block_shape, index_map, pipeline_mode=pl.Buffered(1))


def _rmsnorm_val(x, g):
    ms = jnp.mean(x * x, axis=-1, keepdims=True)
    return (x * lax.rsqrt(ms + EPS)) * g


def _rmsnorm_rows(x_ref, g_ref, dst_ref, dst_off, n_rows):
    g = g_ref[...]

    def body(c, carry):
        r = pl.multiple_of(c * NORM_ROWS, NORM_ROWS)
        x = x_ref[pl.ds(r, NORM_ROWS), :]
        dst_ref[pl.ds(dst_off + r, NORM_ROWS), :] = _rmsnorm_val(x, g).astype(dst_ref.dtype)
        return carry

    lax.fori_loop(0, n_rows // NORM_ROWS, body, 0, unroll=2)


def _rmsnorm_rows_inline(x_ref, g_ref, dst_ref, n_rows):
    g = g_ref[...]
    for c in range(n_rows // NORM_ROWS):
        rows = slice(c * NORM_ROWS, (c + 1) * NORM_ROWS)
        dst_ref[rows, :] = _rmsnorm_val(x_ref[rows, :], g).astype(dst_ref.dtype)


def _norm_proj_kernel(x0_ref, xn_ref, g_ref, *refs, n_blocks, gelu):
    w_refs = refs[:n_blocks]
    b_ref = refs[n_blocks] if gelu else None
    if gelu:
        o_ref, h0_ref, h1_ref, raw0_ref, raw1_ref = refs[-5:]
        raw_refs = (raw0_ref, raw1_ref)
    else:
        o_ref, h0_ref, h1_ref = refs[-3:]
    i = pl.program_id(0)
    tm = x0_ref.shape[0]

    def block_cols(n):
        return slice(n * WEIGHT_COLS, (n + 1) * WEIGHT_COLS)

    def activate(n):
        o_ref[:, block_cols(n)] = jax.nn.gelu(
            raw_refs[n % 2][...] + b_ref[:, block_cols(n)]).astype(o_ref.dtype)

    @pl.when(i == 0)
    def _():
        _rmsnorm_rows(x0_ref, g_ref, h0_ref, 0, tm)

    for parity, (cur_ref, nxt_ref) in enumerate(((h0_ref, h1_ref), (h1_ref, h0_ref))):
        @pl.when(i % 2 == parity)
        def _():
            _rmsnorm_rows_inline(xn_ref, g_ref, nxt_ref, tm)
            for n, w_ref in enumerate(w_refs):
                acc = jnp.dot(cur_ref[...], w_ref[...], preferred_element_type=F32)
                if gelu:
                    raw_refs[n % 2][...] = acc
                    if n > 0:
                        activate(n - 1)
                else:
                    o_ref[:, block_cols(n)] = acc.astype(o_ref.dtype)
            if gelu:
                activate(n_blocks - 1)


def _weight_block_specs(k, n, layer):
    return [_resident((None, k, WEIGHT_COLS), lambda *_, j=j: (layer, 0, j))
            for j in range(n // WEIGHT_COLS)]


def _proj_residual_blocks(a_ref, w_refs, x_ref, o_ref):
    for n, w_ref in enumerate(w_refs):
        cols = slice(n * WEIGHT_COLS, (n + 1) * WEIGHT_COLS)
        o_ref[:, cols] = x_ref[:, cols] + jnp.dot(a_ref[...], w_ref[...],
                                                  preferred_element_type=F32)


def _norm_proj(x, gain, layer_g, w, layer_w, bias, *, out_dtype, tm):
    t, d = x.shape
    n = w.shape[-1]
    nt = t // tm
    nb = n // WEIGHT_COLS
    in_specs = [
        _resident((tm, d), lambda i: (0, 0)),
        pl.BlockSpec((tm, d), lambda i: (jnp.minimum(i + 1, nt - 1), 0)),
        _resident((None, 1, d), lambda i: (layer_g, 0, 0)),
    ] + _weight_block_specs(d, n, layer_w)
    args = [x, x, gain] + [w] * nb
    if bias is not None:
        in_specs.append(_resident((None, 1, n), lambda i: (layer_w, 0, 0)))
        args.append(bias)
    return pl.pallas_call(
        functools.partial(_norm_proj_kernel, n_blocks=nb, gelu=bias is not None),
        grid=(nt,),
        in_specs=in_specs,
        out_specs=pl.BlockSpec((tm, n), lambda i: (i, 0)),
        out_shape=jax.ShapeDtypeStruct((t, n), out_dtype),
        scratch_shapes=[pltpu.VMEM((tm, d), BF16)] * 2
        + ([] if bias is None else [pltpu.VMEM((tm, WEIGHT_COLS), F32)] * 2),
        compiler_params=_params("arbitrary"),
        name="norm_proj" if bias is None else "norm_proj_gelu",
    )(*args)


def _gmlp_out_kernel(u_ref, v0_ref, vnext_ref, vg_ref, ws_ref, bs_ref, *refs, n_blocks):
    wo_refs = refs[:n_blocks]
    x_ref, o_ref, vn0_ref, vn1_ref, y_ref = refs[n_blocks:]
    i = pl.program_id(0)
    tm = u_ref.shape[0]

    @pl.when(i == 0)
    def _():
        _rmsnorm_rows(v0_ref, vg_ref, vn0_ref, 0, tm)

    for parity, (cur_ref, nxt_ref) in enumerate(((vn0_ref, vn1_ref), (vn1_ref, vn0_ref))):
        @pl.when(i % 2 == parity)
        def _():
            _rmsnorm_rows_inline(vnext_ref, vg_ref, nxt_ref, tm)
            for n in range(tm // CHUNK):
                rows = slice(n * CHUNK, (n + 1) * CHUNK)
                for g in range(A_GROUPS):
                    cols = slice(g * A_GROUP_DIM, (g + 1) * A_GROUP_DIM)
                    s = jnp.dot(ws_ref[g], cur_ref[rows, cols], preferred_element_type=F32)
                    s = s + bs_ref[:, g:g + 1]
                    y_ref[rows, cols] = (u_ref[rows, cols] * s).astype(BF16)
            _proj_residual_blocks(y_ref, wo_refs, x_ref, o_ref)


def _gmlp_out(z, x, v_norm, w_s, b_s_t, w_out, layer, *, tm):
    t, d = x.shape
    nt = t // tm
    nb = d // WEIGHT_COLS
    return pl.pallas_call(
        functools.partial(_gmlp_out_kernel, n_blocks=nb),
        grid=(nt,),
        in_specs=[
            pl.BlockSpec((tm, d), lambda i: (i, 0)),
            _resident((tm, d), lambda i: (0, 1)),
            pl.BlockSpec((tm, d), lambda i: (jnp.minimum(i + 1, nt - 1), 1)),
            _resident((None, 1, d), lambda i: (layer, 0, 0)),
            _resident((None, A_GROUPS, CHUNK, CHUNK), lambda i: (layer, 0, 0, 0)),
            _resident((None, CHUNK, A_GROUPS), lambda i: (layer, 0, 0)),
        ] + _weight_block_specs(d, d, layer) + [
            pl.BlockSpec((tm, d), lambda i: (i, 0)),
        ],
        out_specs=pl.BlockSpec((tm, d), lambda i: (i, 0)),
        out_shape=jax.ShapeDtypeStruct((t, d), F32),
        scratch_shapes=[pltpu.VMEM((tm, d), BF16)] * 3,
        compiler_params=_params("arbitrary"),
        name="gmlp_out",
    )(z, z, z, v_norm, w_s, b_s_t, *([w_out] * nb), x)


def _relative_bucket(rel):
    half = N_BUCKETS // 2
    max_exact = half // 2
    ret = (rel > 0).astype(np.int32) * half
    n = np.abs(rel)
    nf = np.maximum(n, 1).astype(np.float32)
    large = max_exact + (np.log(nf / max_exact) / np.log(MAX_DISTANCE / max_exact)
                         * (half - max_exact)).astype(np.int32)
    large = np.minimum(large, half - 1)
    return (ret + np.where(n < max_exact, n, large)).astype(np.int32)


def _band_bucket():
    a = np.arange(BLOCK)[:, None]
    c = np.arange(3 * BLOCK)[None, :]
    rel = c - BLOCK - a
    return np.where(np.abs(rel) <= WINDOW, _relative_bucket(rel), -1).astype(np.int32)


def _bias_table_kernel(bucket_ref, rb_ref, o_ref):
    bucket = bucket_ref[...]
    for h in range(N_HEADS):
        acc = jnp.full(bucket.shape, -jnp.inf, F32)
        for b in range(N_BUCKETS):
            acc = jnp.where(bucket == b, rb_ref[b, h], acc)
        o_ref[h] = acc


def _bias_table(rel_bias):
    return pl.pallas_call(
        _bias_table_kernel,
        in_specs=[pl.BlockSpec(memory_space=pltpu.VMEM),
                  pl.BlockSpec(memory_space=pltpu.SMEM)],
        out_specs=pl.BlockSpec(memory_space=pltpu.VMEM),
        out_shape=jax.ShapeDtypeStruct((N_HEADS, BLOCK, 3 * BLOCK), F32),
        name="bias_table",
    )(jnp.asarray(_band_bucket()), rel_bias)


def _attn_kernel(q_ref, kp_ref, kc_ref, kn_ref, vp_ref, vc_ref, vn_ref, bias_ref, sink_ref,
                 o_ref, *, blocks_per_seq, layer):
    blk = pl.program_id(0) % blocks_per_seq
    col = lax.broadcasted_iota(jnp.int32, (1, 3 * BLOCK), 1)
    first_col = jnp.where(blk == 0, BLOCK, 0)
    end_col = jnp.where(blk == blocks_per_seq - 1, 2 * BLOCK, 3 * BLOCK)
    edge = (col < first_col) | (col >= end_col)
    scale = HEAD_DIM ** -0.5

    def head_cols(h):
        return slice(h * HEAD_DIM, (h + 1) * HEAD_DIM)

    def qk(kh):
        kc = head_cols(kh)
        k = jnp.concatenate([kp_ref[:, kc], kc_ref[:, kc], kn_ref[:, kc]], axis=0)
        q = jnp.concatenate([q_ref[:, head_cols(kh * Q_PER_KV + g)] for g in range(Q_PER_KV)],
                            axis=0)
        return lax.dot_general(q, k, (((1,), (1,)), ((), ())), preferred_element_type=F32)

    def softmax_pv(kh, logits):
        probs = []
        for g in range(Q_PER_KV):
            h = kh * Q_PER_KV + g
            lg = logits[g * BLOCK:(g + 1) * BLOCK] * scale + bias_ref[h]
            lg = jnp.where(edge, -jnp.inf, lg)
            sink = sink_ref[layer, h]
            m = jnp.maximum(jnp.max(lg, axis=-1, keepdims=True), sink)
            p = jnp.exp(lg - m)
            denom = jnp.sum(p, axis=-1, keepdims=True) + jnp.exp(sink - m)
            probs.append((p / denom).astype(BF16))
        kc = head_cols(kh)
        v = jnp.concatenate([vp_ref[:, kc], vc_ref[:, kc], vn_ref[:, kc]], axis=0)
        o = jnp.dot(jnp.concatenate(probs, axis=0), v, preferred_element_type=F32)
        for g in range(Q_PER_KV):
            o_ref[:, head_cols(kh * Q_PER_KV + g)] = o[g * BLOCK:(g + 1) * BLOCK].astype(o_ref.dtype)

    logits = qk(0)
    for kh in range(N_KV_HEADS):
        nxt = qk(kh + 1) if kh + 1 < N_KV_HEADS else None
        softmax_pv(kh, logits)
        logits = nxt


def _attention(qkv, bias, sink, layer, seq_len):
    t = qkv.shape[0]
    nblk = t // BLOCK
    bps = seq_len // BLOCK
    k0 = D_MODEL // KV_DIM
    v0 = k0 + 1

    def kv_spec(col, shift):
        return pl.BlockSpec((BLOCK, KV_DIM),
                            lambda i: (jnp.clip(i + shift, 0, nblk - 1), col))

    return pl.pallas_call(
        functools.partial(_attn_kernel, blocks_per_seq=bps, layer=layer),
        grid=(nblk,),
        in_specs=[
            pl.BlockSpec((BLOCK, D_MODEL), lambda i: (i, 0)),
            kv_spec(k0, -1), kv_spec(k0, 0), kv_spec(k0, 1),
            kv_spec(v0, -1), kv_spec(v0, 0), kv_spec(v0, 1),
            _resident((N_HEADS, BLOCK, 3 * BLOCK), lambda i: (0, 0, 0)),
            pl.BlockSpec(memory_space=pltpu.SMEM),
        ],
        out_specs=pl.BlockSpec((BLOCK, D_MODEL), lambda i: (i, 0)),
        out_shape=jax.ShapeDtypeStruct((t, D_MODEL), BF16),
        compiler_params=_params("parallel"),
        name="band_attention",
    )(qkv, qkv, qkv, qkv, qkv, qkv, qkv, bias, sink)


def _proj_residual_kernel(a_ref, *refs, n_blocks):
    x_ref, o_ref = refs[n_blocks:]
    _proj_residual_blocks(a_ref, refs[:n_blocks], x_ref, o_ref)


def _proj_residual(a, w, layer, x, *, tm):
    t, d = x.shape
    k = a.shape[1]
    nb = d // WEIGHT_COLS
    return pl.pallas_call(
        functools.partial(_proj_residual_kernel, n_blocks=nb),
        grid=(t // tm,),
        in_specs=[pl.BlockSpec((tm, k), lambda i: (i, 0))]
        + _weight_block_specs(k, d, layer)
        + [pl.BlockSpec((tm, d), lambda i: (i, 0))],
        out_specs=pl.BlockSpec((tm, d), lambda i: (i, 0)),
        out_shape=jax.ShapeDtypeStruct((t, d), F32),
        compiler_params=_params("parallel"),
        name="proj_residual",
    )(a, *([w] * nb), x)


def _ffn_kernel(x_ref, xp_ref, xn_ref, g_ref, wg_ref, wu_ref, cwg_ref, cwu_ref, cbg_ref,
                cbu_ref, wo_ref, gf_ref, o_ref, h_ref, *z_refs, tiles_per_seq, final_norm):
    i = pl.program_id(0)
    c = pl.program_id(1)
    tm = x_ref.shape[0]
    tc = wg_ref.shape[1]
    n_parts = len(z_refs) // 2
    tp = tc // n_parts

    @pl.when(c == 0)
    def _():
        _rmsnorm_rows(x_ref, g_ref, h_ref, HALO, tm)
        pos = i % tiles_per_seq
        hp = _rmsnorm_val(xp_ref[...], g_ref[...])
        hn = _rmsnorm_val(xn_ref[...], g_ref[...])
        h_ref[0:HALO, :] = jnp.where(pos == 0, 0.0, hp).astype(BF16)
        h_ref[HALO + tm:, :] = jnp.where(pos == tiles_per_seq - 1, 0.0, hn).astype(BF16)
        o_ref[...] = x_ref[...]

    h = h_ref[...]
    part_cols = [slice(p * tp, (p + 1) * tp) for p in range(n_parts)]
    for p, cols in enumerate(part_cols):
        z_refs[2 * p][...] = jnp.dot(h, wg_ref[:, cols], preferred_element_type=F32)
    for p, cols in enumerate(part_cols):
        z_refs[2 * p + 1][...] = jnp.dot(h, wu_ref[:, cols], preferred_element_type=F32)

    def conv(z_ref, cw_ref, cb_ref, cols):
        return (z_ref[pl.ds(HALO - 1, tm), :] * cw_ref[0:1, cols]
                + z_ref[pl.ds(HALO, tm), :] * cw_ref[1:2, cols]
                + z_ref[pl.ds(HALO + 1, tm), :] * cw_ref[2:3, cols]
                + cb_ref[:, cols])

    gates = [jax.nn.silu(conv(z_refs[2 * p], cwg_ref, cbg_ref, cols))
             for p, cols in enumerate(part_cols)]
    for p, cols in enumerate(part_cols):
        up = conv(z_refs[2 * p + 1], cwu_ref, cbu_ref, cols)
        act = (gates[p] * up).astype(BF16)
        o_ref[...] += jnp.dot(act, wo_ref[cols, :], preferred_element_type=F32)

    if final_norm:
        @pl.when(c == pl.num_programs(1) - 1)
        def _():
            _rmsnorm_rows(o_ref, gf_ref, o_ref, 0, tm)


def _ffn(x, gain, w_in, conv_w, conv_b, w_out, final_gain, layer, seq_len, *, tm, tc,
         final_norm):
    t, d = x.shape
    nch = D_FF // tc
    hb = tm // HALO
    nhb = t // HALO
    return pl.pallas_call(
        functools.partial(_ffn_kernel, tiles_per_seq=seq_len // tm, final_norm=final_norm),
        grid=(t // tm, nch),
        in_specs=[
            pl.BlockSpec((tm, d), lambda i, c: (i, 0)),
            pl.BlockSpec((HALO, d), lambda i, c: (jnp.maximum(i * hb - 1, 0), 0)),
            pl.BlockSpec((HALO, d), lambda i, c: (jnp.minimum((i + 1) * hb, nhb - 1), 0)),
            _resident((None, 1, d), lambda i, c: (layer, 0, 0)),
            pl.BlockSpec((None, d, tc), lambda i, c: (layer, 0, c)),
            pl.BlockSpec((None, d, tc), lambda i, c: (layer, 0, nch + c)),
            pl.BlockSpec((None, 3, tc), lambda i, c: (layer, 0, c)),
            pl.BlockSpec((None, 3, tc), lambda i, c: (layer, 0, nch + c)),
            pl.BlockSpec((None, 1, tc), lambda i, c: (layer, 0, c)),
            pl.BlockSpec((None, 1, tc), lambda i, c: (layer, 0, nch + c)),
            pl.BlockSpec((None, tc, d), lambda i, c: (layer, c, 0)),
            _resident((1, d), lambda i, c: (0, 0)),
        ],
        out_specs=pl.BlockSpec((tm, d), lambda i, c: (i, 0)),
        out_shape=jax.ShapeDtypeStruct((t, d), F32),
        scratch_shapes=[pltpu.VMEM((tm + 2 * HALO, d), BF16)]
        + [pltpu.VMEM((tm + 2 * HALO, tc // FFN_PARTS), F32)] * (2 * FFN_PARTS),
        compiler_params=_params("parallel", "arbitrary"),
        name="conv_ffn",
    )(x, x, x, gain, w_in, w_in, conv_w, conv_w, conv_b, conv_b, w_out, final_gain)


def _trunk(x3, p):
    b, s, d = x3.shape
    x = x3.reshape(b * s, d)
    for i in range(DEPTH):
        j = i // 2
        if i % 2 == 0:
            z = _norm_proj(x, p["mix_norm"], i, p["a_w_in"], j, p["a_b_in"],
                           out_dtype=F32, tm=512)
            x = _gmlp_out(z, x, p["a_v_norm"], p["a_w_s"], p["a_b_s_t"], p["a_w_out"], j, tm=512)
        else:
            qkv = _norm_proj(x, p["mix_norm"], i, p["b_w_qkv"], j, None,
                             out_dtype=BF16, tm=512)
            o = _attention(qkv, p["bias"], p["b_sink"], j, s)
            x = _proj_residual(o, p["b_w_out"], j, x, tm=512)
        x = _ffn(x, p["ffn_norm"], p["f_w_in"], p["f_conv_w"], p["f_conv_b"], p["f_w_out"],
                 p["final_norm"], i, s, tm=1024, tc=512, final_norm=i == DEPTH - 1)
    return x.reshape(b, s, d)


def kernel(x_prompt, x_sample, rel_bias, mix_norm, ffn_norm, final_norm, a_w_in, a_b_in, a_v_norm, a_w_s, a_b_s, a_w_out, b_w_qkv, b_sink, b_w_out, f_w_in, f_conv_w, f_conv_b, f_w_out):
    row = lambda a: a[:, None, :]
    p = {
        "mix_norm": row(mix_norm), "ffn_norm": row(ffn_norm), "final_norm": final_norm[None, :],
        "a_w_in": a_w_in.astype(BF16), "a_b_in": row(a_b_in), "a_v_norm": row(a_v_norm),
        "a_w_s": a_w_s.astype(BF16), "a_b_s_t": jnp.swapaxes(a_b_s, 1, 2),
        "a_w_out": a_w_out.astype(BF16),
        "b_w_qkv": b_w_qkv.astype(BF16), "b_sink": b_sink, "b_w_out": b_w_out.astype(BF16),
        "f_w_in": f_w_in.astype(BF16), "f_conv_w": f_conv_w, "f_conv_b": row(f_conv_b),
        "f_w_out": f_w_out.astype(BF16),
        "bias": _bias_table(rel_bias),
    }
    return (_trunk(x_prompt, p), _trunk(x_sample, p))
```

```python
import functools

import numpy as np
import jax
import jax.numpy as jnp
from jax import lax
from jax.experimental import pallas as pl
from jax.experimental.pallas import tpu as pltpu

F32 = jnp.float32
BF16 = jnp.bfloat16

D_MODEL = 2048
DEPTH = 4
CHUNK = 128
A_GROUPS = 8
A_GROUP_DIM = D_MODEL // A_GROUPS
HEAD_DIM = 128
N_HEADS = D_MODEL // HEAD_DIM
N_KV_HEADS = 4
Q_PER_KV = N_HEADS // N_KV_HEADS
KV_DIM = N_KV_HEADS * HEAD_DIM
WINDOW = 128
BLOCK = 128
N_BUCKETS = 32
MAX_DISTANCE = 128
D_FF = 5632
EPS = 1e-6
LOG2E = 1.4426950408889634

VMEM_LIMIT_BYTES = 60 * 1024 * 1024
F32_SUBLANES = 8
BF16_SUBLANES = 16
HALO = BF16_SUBLANES
NORM_ROWS = 64
WEIGHT_COLS = 512
FFN_PARTS = 2


def _params(*sem):
    return pltpu.CompilerParams(dimension_semantics=sem,
                                vmem_limit_bytes=VMEM_LIMIT_BYTES)


def _resident(block_shape, index_map):
    return pl.BlockSpec(block_shape, index_map, pipeline_mode=pl.Buffered(1))


def _rmsnorm_val(x, g):
    ms = jnp.mean(x * x, axis=-1, keepdims=True)
    return (x * lax.rsqrt(ms + EPS)) * g


def _rmsnorm_rows(x_ref, g_ref, dst_ref, dst_off, n_rows):
    g = g_ref[...]

    def body(c, carry):
        r = pl.multiple_of(c * NORM_ROWS, NORM_ROWS)
        x = x_ref[pl.ds(r, NORM_ROWS), :]
        dst_ref[pl.ds(dst_off + r, NORM_ROWS), :] = _rmsnorm_val(x, g).astype(dst_ref.dtype)
        return carry

    lax.fori_loop(0, n_rows // NORM_ROWS, body, 0, unroll=2)


def _rmsnorm_rows_inline(x_ref, g_ref, dst_ref, n_rows):
    g = g_ref[...]
    for c in range(n_rows // NORM_ROWS):
        rows = slice(c * NORM_ROWS, (c + 1) * NORM_ROWS)
        dst_ref[rows, :] = _rmsnorm_val(x_ref[rows, :], g).astype(dst_ref.dtype)


def _norm_proj_kernel(x0_ref, xn_ref, g_ref, *refs, n_blocks, gelu):
    w_refs = refs[:n_blocks]
    b_ref = refs[n_blocks] if gelu else None
    if gelu:
        o_ref, h0_ref, h1_ref, raw0_ref, raw1_ref = refs[-5:]
        raw_refs = (raw0_ref, raw1_ref)
    else:
        o_ref, h0_ref, h1_ref = refs[-3:]
    i = pl.program_id(0)
    tm = x0_ref.shape[0]

    def block_cols(n):
        return slice(n * WEIGHT_COLS, (n + 1) * WEIGHT_COLS)

    def activate(n):
        o_ref[:, block_cols(n)] = jax.nn.gelu(
            raw_refs[n % 2][...] + b_ref[:, block_cols(n)]).astype(o_ref.dtype)

    @pl.when(i == 0)
    def _():
        _rmsnorm_rows(x0_ref, g_ref, h0_ref, 0, tm)

    for parity, (cur_ref, nxt_ref) in enumerate(((h0_ref, h1_ref), (h1_ref, h0_ref))):
        @pl.when(i % 2 == parity)
        def _():
            _rmsnorm_rows_inline(xn_ref, g_ref, nxt_ref, tm)
            for n, w_ref in enumerate(w_refs):
                acc = jnp.dot(cur_ref[...], w_ref[...], preferred_element_type=F32)
                if gelu:
                    raw_refs[n % 2][...] = acc
                    if n > 0:
                        activate(n - 1)
                else:
                    o_ref[:, block_cols(n)] = acc.astype(o_ref.dtype)
            if gelu:
                activate(n_blocks - 1)


def _weight_block_specs(k, n, layer):
    return [_resident((None, k, WEIGHT_COLS), lambda *_, j=j: (layer, 0, j))
            for j in range(n // WEIGHT_COLS)]


def _proj_residual_blocks(a_ref, w_refs, x_ref, o_ref):
    for n, w_ref in enumerate(w_refs):
        cols = slice(n * WEIGHT_COLS, (n + 1) * WEIGHT_COLS)
        o_ref[:, cols] = x_ref[:, cols] + jnp.dot(a_ref[...], w_ref[...],
                                                  preferred_element_type=F32)


def _norm_proj(x, gain, layer_g, w, layer_w, bias, *, out_dtype, tm):
    t, d = x.shape
    n = w.shape[-1]
    nt = t // tm
    nb = n // WEIGHT_COLS
    in_specs = [
        _resident((tm, d), lambda i: (0, 0)),
        pl.BlockSpec((tm, d), lambda i: (jnp.minimum(i + 1, nt - 1), 0)),
        _resident((None, 1, d), lambda i: (layer_g, 0, 0)),
    ] + _weight_block_specs(d, n, layer_w)
    args = [x, x, gain] + [w] * nb
    if bias is not None:
        in_specs.append(_resident((None, 1, n), lambda i: (layer_w, 0, 0)))
        args.append(bias)
    return pl.pallas_call(
        functools.partial(_norm_proj_kernel, n_blocks=nb, gelu=bias is not None),
        grid=(nt,),
        in_specs=in_specs,
        out_specs=pl.BlockSpec((tm, n), lambda i: (i, 0)),
        out_shape=jax.ShapeDtypeStruct((t, n), out_dtype),
        scratch_shapes=[pltpu.VMEM((tm, d), BF16)] * 2
        + ([] if bias is None else [pltpu.VMEM((tm, WEIGHT_COLS), F32)] * 2),
        compiler_params=_params("arbitrary"),
        name="norm_proj" if bias is None else "norm_proj_gelu",
    )(*args)


def _gmlp_out_kernel(u_ref, v0_ref, vnext_ref, vg_ref, ws_ref, bs_ref, *refs, n_blocks):
    wo_refs = refs[:n_blocks]
    x_ref, o_ref, vn0_ref, vn1_ref, y_ref = refs[n_blocks:]
    i = pl.program_id(0)
    tm = u_ref.shape[0]

    @pl.when(i == 0)
    def _():
        _rmsnorm_rows(v0_ref, vg_ref, vn0_ref, 0, tm)

    for parity, (cur_ref, nxt_ref) in enumerate(((vn0_ref, vn1_ref), (vn1_ref, vn0_ref))):
        @pl.when(i % 2 == parity)
        def _():
            _rmsnorm_rows_inline(vnext_ref, vg_ref, nxt_ref, tm)
            for n in range(tm // CHUNK):
                rows = slice(n * CHUNK, (n + 1) * CHUNK)
                for g in range(A_GROUPS):
                    cols = slice(g * A_GROUP_DIM, (g + 1) * A_GROUP_DIM)
                    s = jnp.dot(ws_ref[g], cur_ref[rows, cols], preferred_element_type=F32)
                    s = s + bs_ref[:, g:g + 1]
                    y_ref[rows, cols] = (u_ref[rows, cols] * s).astype(BF16)
            _proj_residual_blocks(y_ref, wo_refs, x_ref, o_ref)


def _gmlp_out(z, x, v_norm, w_s, b_s_t, w_out, layer, *, tm):
    t, d = x.shape
    nt = t // tm
    nb = d // WEIGHT_COLS
    return pl.pallas_call(
        functools.partial(_gmlp_out_kernel, n_blocks=nb),
        grid=(nt,),
        in_specs=[
            pl.BlockSpec((tm, d), lambda i: (i, 0)),
            _resident((tm, d), lambda i: (0, 1)),
            pl.BlockSpec((tm, d), lambda i: (jnp.minimum(i + 1, nt - 1), 1)),
            _resident((None, 1, d), lambda i: (layer, 0, 0)),
            _resident((None, A_GROUPS, CHUNK, CHUNK), lambda i: (layer, 0, 0, 0)),
            _resident((None, CHUNK, A_GROUPS), lambda i: (layer, 0, 0)),
        ] + _weight_block_specs(d, d, layer) + [
            pl.BlockSpec((tm, d), lambda i: (i, 0)),
        ],
        out_specs=pl.BlockSpec((tm, d), lambda i: (i, 0)),
        out_shape=jax.ShapeDtypeStruct((t, d), F32),
        scratch_shapes=[pltpu.VMEM((tm, d), BF16)] * 3,
        compiler_params=_params("arbitrary"),
        name="gmlp_out",
    )(z, z, z, v_norm, w_s, b_s_t, *([w_out] * nb), x)


def _relative_bucket(rel):
    half = N_BUCKETS // 2
    max_exact = half // 2
    ret = (rel > 0).astype(np.int32) * half
    n = np.abs(rel)
    nf = np.maximum(n, 1).astype(np.float32)
    large = max_exact + (np.log(nf / max_exact) / np.log(MAX_DISTANCE / max_exact)
                         * (half - max_exact)).astype(np.int32)
    large = np.minimum(large, half - 1)
    return (ret + np.where(n < max_exact, n, large)).astype(np.int32)


def _band_bucket():
    a = np.arange(BLOCK)[:, None]
    c = np.arange(3 * BLOCK)[None, :]
    rel = c - BLOCK - a
    return np.where(np.abs(rel) <= WINDOW, _relative_bucket(rel), -1).astype(np.int32)


def _bias_table_kernel(bucket_ref, rb_ref, o_ref):
    bucket = bucket_ref[...]
    for h in range(N_HEADS):
        acc = jnp.full(bucket.shape, -jnp.inf, F32)
        for b in range(N_BUCKETS):
            acc = jnp.where(bucket == b, rb_ref[b, h] * LOG2E, acc)
        o_ref[h] = acc


def _bias_table(rel_bias):
    return pl.pallas_call(
        _bias_table_kernel,
        in_specs=[pl.BlockSpec(memory_space=pltpu.VMEM),
                  pl.BlockSpec(memory_space=pltpu.SMEM)],
        out_specs=pl.BlockSpec(memory_space=pltpu.VMEM),
        out_shape=jax.ShapeDtypeStruct((N_HEADS, BLOCK, 3 * BLOCK), F32),
        name="bias_table",
    )(jnp.asarray(_band_bucket()), rel_bias)


def _attn_kernel(q_ref, kp_ref, kc_ref, kn_ref, vp_ref, vc_ref, vn_ref, bias_ref, sink_ref,
                 o_ref, *, blocks_per_seq, layer):
    blk = pl.program_id(0) % blocks_per_seq
    col = lax.broadcasted_iota(jnp.int32, (1, 3 * BLOCK), 1)
    first_col = jnp.where(blk == 0, BLOCK, 0)
    end_col = jnp.where(blk == blocks_per_seq - 1, 2 * BLOCK, 3 * BLOCK)
    edge = (col < first_col) | (col >= end_col)
    scale = HEAD_DIM ** -0.5 * LOG2E

    def head_cols(h):
        return slice(h * HEAD_DIM, (h + 1) * HEAD_DIM)

    def qk(kh):
        kc = head_cols(kh)
        k = jnp.concatenate([kp_ref[:, kc], kc_ref[:, kc], kn_ref[:, kc]], axis=0)
        q = jnp.concatenate([q_ref[:, head_cols(kh * Q_PER_KV + g)] for g in range(Q_PER_KV)],
                            axis=0)
        return lax.dot_general(q, k, (((1,), (1,)), ((), ())), preferred_element_type=F32)

    def softmax_pv(kh, logits):
        probs = []
        for g in range(Q_PER_KV):
            h = kh * Q_PER_KV + g
            lg = logits[g * BLOCK:(g + 1) * BLOCK] * scale + bias_ref[h]
            lg = jnp.where(edge, -jnp.inf, lg)
            sink = sink_ref[layer, h] * LOG2E
            m = jnp.maximum(jnp.max(lg, axis=-1, keepdims=True), sink)
            p = jnp.exp2(lg - m)
            denom = jnp.sum(p, axis=-1, keepdims=True) + jnp.exp2(sink - m)
            probs.append((p / denom).astype(BF16))
        kc = head_cols(kh)
        v = jnp.concatenate([vp_ref[:, kc], vc_ref[:, kc], vn_ref[:, kc]], axis=0)
        o = jnp.dot(jnp.concatenate(probs, axis=0), v, preferred_element_type=F32)
        for g in range(Q_PER_KV):
            o_ref[:, head_cols(kh * Q_PER_KV + g)] = o[g * BLOCK:(g + 1) * BLOCK].astype(o_ref.dtype)

    logits = qk(0)
    for kh in range(N_KV_HEADS):
        nxt = qk(kh + 1) if kh + 1 < N_KV_HEADS else None
        softmax_pv(kh, logits)
        logits = nxt


def _attention(qkv, bias, sink, layer, seq_len):
    t = qkv.shape[0]
    nblk = t // BLOCK
    bps = seq_len // BLOCK
    k0 = D_MODEL // KV_DIM
    v0 = k0 + 1

    def kv_spec(col, shift):
        return pl.BlockSpec((BLOCK, KV_DIM),
                            lambda i: (jnp.clip(i + shift, 0, nblk - 1), col))

    return pl.pallas_call(
        functools.partial(_attn_kernel, blocks_per_seq=bps, layer=layer),
        grid=(nblk,),
        in_specs=[
            pl.BlockSpec((BLOCK, D_MODEL), lambda i: (i, 0)),
            kv_spec(k0, -1), kv_spec(k0, 0), kv_spec(k0, 1),
            kv_spec(v0, -1), kv_spec(v0, 0), kv_spec(v0, 1),
            _resident((N_HEADS, BLOCK, 3 * BLOCK), lambda i: (0, 0, 0)),
            pl.BlockSpec(memory_space=pltpu.SMEM),
        ],
        out_specs=pl.BlockSpec((BLOCK, D_MODEL), lambda i: (i, 0)),
        out_shape=jax.ShapeDtypeStruct((t, D_MODEL), BF16),
        compiler_params=_params("parallel"),
        name="band_attention",
    )(qkv, qkv, qkv, qkv, qkv, qkv, qkv, bias, sink)


def _proj_residual_kernel(a_ref, *refs, n_blocks):
    x_ref, o_ref = refs[n_blocks:]
    _proj_residual_blocks(a_ref, refs[:n_blocks], x_ref, o_ref)


def _proj_residual(a, w, layer, x, *, tm):
    t, d = x.shape
    k = a.shape[1]
    nb = d // WEIGHT_COLS
    return pl.pallas_call(
        functools.partial(_proj_residual_kernel, n_blocks=nb),
        grid=(t // tm,),
        in_specs=[pl.BlockSpec((tm, k), lambda i: (i, 0))]
        + _weight_block_specs(k, d, layer)
        + [pl.BlockSpec((tm, d), lambda i: (i, 0))],
        out_specs=pl.BlockSpec((tm, d), lambda i: (i, 0)),
        out_shape=jax.ShapeDtypeStruct((t, d), F32),
        compiler_params=_params("parallel"),
        name="proj_residual",
    )(a, *([w] * nb), x)


def _ffn_kernel(x_ref, xp_ref, xn_ref, g_ref, wg_ref, wu_ref, cwg_ref, cwu_ref, cbg_ref,
                cbu_ref, wo_ref, gf_ref, o_ref, h_ref, *z_refs, tiles_per_seq, final_norm):
    i = pl.program_id(0)
    c = pl.program_id(1)
    tm = x_ref.shape[0]
    tc = wg_ref.shape[1]
    n_parts = len(z_refs) // 2
    tp = tc // n_parts

    @pl.when(c == 0)
    def _():
        _rmsnorm_rows(x_ref, g_ref, h_ref, HALO, tm)
        pos = i % tiles_per_seq
        hp = jnp.where(pos == 0, 0.0, _rmsnorm_val(xp_ref[...], g_ref[...]))
        hn = jnp.where(pos == tiles_per_seq - 1, 0.0, _rmsnorm_val(xn_ref[...], g_ref[...]))
        halo_row = lax.broadcasted_iota(jnp.int32, (HALO, 1), 0)
        h_ref[0:HALO, :] = jnp.where(halo_row == 0, hn, hp).astype(BF16)

    part_cols = [slice(p * tp, (p + 1) * tp) for p in range(n_parts)]

    def up_proj(z_ref, w_ref, cols):
        z_ref[0:HALO + tm, :] = jnp.dot(h_ref[...], w_ref[:, cols], preferred_element_type=F32)
        z_ref[HALO + tm:HALO + tm + F32_SUBLANES, :] = z_ref[0:F32_SUBLANES, :]

    def conv(z_ref, cw_ref, cb_ref, cols):
        return (z_ref[pl.ds(HALO - 1, tm), :] * cw_ref[0:1, cols]
                + z_ref[pl.ds(HALO, tm), :] * cw_ref[1:2, cols]
                + z_ref[pl.ds(HALO + 1, tm), :] * cw_ref[2:3, cols]
                + cb_ref[:, cols])

    def chunk(first):
        for p, cols in enumerate(part_cols):
            up_proj(z_refs[2 * p], wg_ref, cols)
        for p, cols in enumerate(part_cols):
            up_proj(z_refs[2 * p + 1], wu_ref, cols)
        gates = [jax.nn.silu(conv(z_refs[2 * p], cwg_ref, cbg_ref, cols))
                 for p, cols in enumerate(part_cols)]
        for p, cols in enumerate(part_cols):
            up = conv(z_refs[2 * p + 1], cwu_ref, cbu_ref, cols)
            act = (gates[p] * up).astype(BF16)
            d = jnp.dot(act, wo_ref[cols, :], preferred_element_type=F32)
            if first and p == 0:
                o_ref[...] = x_ref[...] + d
            else:
                o_ref[...] += d

    @pl.when(c == 0)
    def _():
        chunk(True)

    @pl.when(c > 0)
    def _():
        chunk(False)

    if final_norm:
        @pl.when(c == pl.num_programs(1) - 1)
        def _():
            _rmsnorm_rows(o_ref, gf_ref, o_ref, 0, tm)


def _ffn(x, gain, w_in, conv_w, conv_b, w_out, final_gain, layer, seq_len, *, tm, tc,
         final_norm):
    t, d = x.shape
    nch = D_FF // tc
    hb = tm // HALO
    nhb = t // HALO
    return pl.pallas_call(
        functools.partial(_ffn_kernel, tiles_per_seq=seq_len // tm, final_norm=final_norm),
        grid=(t // tm, nch),
        in_specs=[
            pl.BlockSpec((tm, d), lambda i, c: (i, 0)),
            pl.BlockSpec((HALO, d), lambda i, c: (jnp.maximum(i * hb - 1, 0), 0)),
            pl.BlockSpec((HALO, d), lambda i, c: (jnp.minimum((i + 1) * hb, nhb - 1), 0)),
            _resident((None, 1, d), lambda i, c: (layer, 0, 0)),
            pl.BlockSpec((None, d, tc), lambda i, c: (layer, 0, c)),
            pl.BlockSpec((None, d, tc), lambda i, c: (layer, 0, nch + c)),
            pl.BlockSpec((None, 3, tc), lambda i, c: (layer, 0, c)),
            pl.BlockSpec((None, 3, tc), lambda i, c: (layer, 0, nch + c)),
            pl.BlockSpec((None, 1, tc), lambda i, c: (layer, 0, c)),
            pl.BlockSpec((None, 1, tc), lambda i, c: (layer, 0, nch + c)),
            pl.BlockSpec((None, tc, d), lambda i, c: (layer, c, 0)),
            _resident((1, d), lambda i, c: (0, 0)),
        ],
        out_specs=pl.BlockSpec((tm, d), lambda i, c: (i, 0)),
        out_shape=jax.ShapeDtypeStruct((t, d), F32),
        scratch_shapes=[pltpu.VMEM((tm + HALO, d), BF16)]
        + [pltpu.VMEM((tm + HALO + F32_SUBLANES, tc // FFN_PARTS), F32)] * (2 * FFN_PARTS),
        compiler_params=_params("parallel", "arbitrary"),
        name="conv_ffn",
    )(x, x, x, gain, w_in, w_in, conv_w, conv_w, conv_b, conv_b, w_out, final_gain)


def _trunk(x3, p):
    b, s, d = x3.shape
    x = x3.reshape(b * s, d)
    for i in range(DEPTH):
        j = i // 2
        if i % 2 == 0:
            z = _norm_proj(x, p["mix_norm"], i, p["a_w_in"], j, p["a_b_in"],
                           out_dtype=F32, tm=512)
            x = _gmlp_out(z, x, p["a_v_norm"], p["a_w_s"], p["a_b_s_t"], p["a_w_out"], j, tm=512)
        else:
            qkv = _norm_proj(x, p["mix_norm"], i, p["b_w_qkv"], j, None,
                             out_dtype=BF16, tm=512)
            o = _attention(qkv, p["bias"], p["b_sink"], j, s)
            x = _proj_residual(o, p["b_w_out"], j, x, tm=512)
        x = _ffn(x, p["ffn_norm"], p["f_w_in"], p["f_conv_w"], p["f_conv_b"], p["f_w_out"],
                 p["final_norm"], i, s, tm=1024, tc=512, final_norm=i == DEPTH - 1)
    return x.reshape(b, s, d)


def kernel(x_prompt, x_sample, rel_bias, mix_norm, ffn_norm, final_norm, a_w_in, a_b_in, a_v_norm, a_w_s, a_b_s, a_w_out, b_w_qkv, b_sink, b_w_out, f_w_in, f_conv_w, f_conv_b, f_w_out):
    row = lambda a: a[:, None, :]
    p = {
        "mix_norm": row(mix_norm), "ffn_norm": row(ffn_norm), "final_norm": final_norm[None, :],
        "a_w_in": a_w_in.astype(BF16), "a_b_in": row(a_b_in), "a_v_norm": row(a_v_norm),
        "a_w_s": a_w_s.astype(BF16), "a_b_s_t": jnp.swapaxes(a_b_s, 1, 2),
        "a_w_out": a_w_out.astype(BF16),
        "b_w_qkv": b_w_qkv.astype(BF16), "b_sink": b_sink, "b_w_out": b_w_out.astype(BF16),
        "f_w_in": f_w_in.astype(BF16), "f_conv_w": f_conv_w, "f_conv_b": row(f_conv_b),
        "f_w_out": f_w_out.astype(BF16),
        "bias": _bias_table(rel_bias),
    }
    return (_trunk(x_prompt, p), _trunk(x_sample, p))
```

```python
import functools

import numpy as np
import jax
import jax.numpy as jnp
from jax import lax
from jax.experimental import pallas as pl
from jax.experimental.pallas import tpu as pltpu

F32 = jnp.float32
BF16 = jnp.bfloat16

D_MODEL = 2048
DEPTH = 4
CHUNK = 128
A_GROUPS = 8
A_GROUP_DIM = D_MODEL // A_GROUPS
HEAD_DIM = 128
N_HEADS = D_MODEL // HEAD_DIM
N_KV_HEADS = 4
Q_PER_KV = N_HEADS // N_KV_HEADS
KV_DIM = N_KV_HEADS * HEAD_DIM
WINDOW = 128
BLOCK = 128
Q_BLOCKS = 2
N_BUCKETS = 32
MAX_DISTANCE = 128
D_FF = 5632
EPS = 1e-6
LOG2E = 1.4426950408889634

VMEM_LIMIT_BYTES = 60 * 1024 * 1024
F32_SUBLANES = 8
BF16_SUBLANES = 16
HALO = BF16_SUBLANES
NORM_ROWS = 64
WEIGHT_COLS = 512
FFN_PARTS = 2


def _params(*sem):
    return pltpu.CompilerParams(dimension_semantics=sem,
                                vmem_limit_bytes=VMEM_LIMIT_BYTES)


def _resident(block_shape, index_map):
    return pl.BlockSpec(block_shape, index_map, pipeline_mode=pl.Buffered(1))


def _rmsnorm_val(x, g):
    ms = jnp.mean(x * x, axis=-1, keepdims=True)
    return (x * lax.rsqrt(ms + EPS)) * g


def _rmsnorm_rows(x_ref, g_ref, dst_ref, dst_off, n_rows):
    g = g_ref[...]

    def body(c, carry):
        r = pl.multiple_of(c * NORM_ROWS, NORM_ROWS)
        x = x_ref[pl.ds(r, NORM_ROWS), :]
        dst_ref[pl.ds(dst_off + r, NORM_ROWS), :] = _rmsnorm_val(x, g).astype(dst_ref.dtype)
        return carry

    lax.fori_loop(0, n_rows // NORM_ROWS, body, 0, unroll=4)


def _rmsnorm_rows_inline(x_ref, g_ref, dst_ref, n_rows):
    g = g_ref[...]
    for c in range(n_rows // NORM_ROWS):
        rows = slice(c * NORM_ROWS, (c + 1) * NORM_ROWS)
        dst_ref[rows, :] = _rmsnorm_val(x_ref[rows, :], g).astype(dst_ref.dtype)


def _norm_proj_kernel(x0_ref, xn_ref, g_ref, *refs, n_blocks, gelu):
    w_refs = refs[:n_blocks]
    b_ref = refs[n_blocks] if gelu else None
    if gelu:
        o_ref, h0_ref, h1_ref, raw0_ref, raw1_ref = refs[-5:]
        raw_refs = (raw0_ref, raw1_ref)
    else:
        o_ref, h0_ref, h1_ref = refs[-3:]
    i = pl.program_id(0)
    tm = x0_ref.shape[0]

    def block_cols(n):
        return slice(n * WEIGHT_COLS, (n + 1) * WEIGHT_COLS)

    def activate(n):
        o_ref[:, block_cols(n)] = jax.nn.gelu(
            raw_refs[n % 2][...] + b_ref[:, block_cols(n)]).astype(o_ref.dtype)

    @pl.when(i == 0)
    def _():
        _rmsnorm_rows(x0_ref, g_ref, h0_ref, 0, tm)

    for parity, (cur_ref, nxt_ref) in enumerate(((h0_ref, h1_ref), (h1_ref, h0_ref))):
        @pl.when(i % 2 == parity)
        def _():
            _rmsnorm_rows_inline(xn_ref, g_ref, nxt_ref, tm)
            for n, w_ref in enumerate(w_refs):
                acc = jnp.dot(cur_ref[...], w_ref[...], preferred_element_type=F32)
                if gelu:
                    raw_refs[n % 2][...] = acc
                    if n > 0:
                        activate(n - 1)
                else:
                    o_ref[:, block_cols(n)] = acc.astype(o_ref.dtype)
            if gelu:
                activate(n_blocks - 1)


def _weight_block_specs(k, n, layer):
    return [_resident((None, k, WEIGHT_COLS), lambda *_, j=j: (layer, 0, j))
            for j in range(n // WEIGHT_COLS)]


def _proj_residual_blocks(a_ref, w_refs, x_ref, o_ref):
    for n, w_ref in enumerate(w_refs):
        cols = slice(n * WEIGHT_COLS, (n + 1) * WEIGHT_COLS)
        o_ref[:, cols] = x_ref[:, cols] + jnp.dot(a_ref[...], w_ref[...],
                                                  preferred_element_type=F32)


def _norm_proj(x, gain, layer_g, w, layer_w, bias, *, out_dtype, tm):
    t, d = x.shape
    n = w.shape[-1]
    nt = t // tm
    nb = n // WEIGHT_COLS
    in_specs = [
        _resident((tm, d), lambda i: (0, 0)),
        pl.BlockSpec((tm, d), lambda i: (jnp.minimum(i + 1, nt - 1), 0)),
        _resident((None, 1, d), lambda i: (layer_g, 0, 0)),
    ] + _weight_block_specs(d, n, layer_w)
    args = [x, x, gain] + [w] * nb
    if bias is not None:
        in_specs.append(_resident((None, 1, n), lambda i: (layer_w, 0, 0)))
        args.append(bias)
    return pl.pallas_call(
        functools.partial(_norm_proj_kernel, n_blocks=nb, gelu=bias is not None),
        grid=(nt,),
        in_specs=in_specs,
        out_specs=pl.BlockSpec((tm, n), lambda i: (i, 0)),
        out_shape=jax.ShapeDtypeStruct((t, n), out_dtype),
        scratch_shapes=[pltpu.VMEM((tm, d), BF16)] * 2
        + ([] if bias is None else [pltpu.VMEM((tm, WEIGHT_COLS), F32)] * 2),
        compiler_params=_params("arbitrary"),
        name="norm_proj" if bias is None else "norm_proj_gelu",
    )(*args)


def _gmlp_out_kernel(u_ref, v0_ref, vnext_ref, vg_ref, ws_ref, bs_ref, *refs, n_blocks):
    wo_refs = refs[:n_blocks]
    x_ref, o_ref, vn0_ref, vn1_ref, y_ref = refs[n_blocks:]
    i = pl.program_id(0)
    tm = u_ref.shape[0]

    @pl.when(i == 0)
    def _():
        _rmsnorm_rows(v0_ref, vg_ref, vn0_ref, 0, tm)

    for parity, (cur_ref, nxt_ref) in enumerate(((vn0_ref, vn1_ref), (vn1_ref, vn0_ref))):
        @pl.when(i % 2 == parity)
        def _():
            _rmsnorm_rows_inline(vnext_ref, vg_ref, nxt_ref, tm)
            for n in range(tm // CHUNK):
                rows = slice(n * CHUNK, (n + 1) * CHUNK)
                for g in range(A_GROUPS):
                    cols = slice(g * A_GROUP_DIM, (g + 1) * A_GROUP_DIM)
                    s = jnp.dot(ws_ref[g], cur_ref[rows, cols], preferred_element_type=F32)
                    s = s + bs_ref[:, g:g + 1]
                    y_ref[rows, cols] = (u_ref[rows, cols] * s).astype(BF16)
            _proj_residual_blocks(y_ref, wo_refs, x_ref, o_ref)


def _gmlp_out(z, x, v_norm, w_s, b_s_t, w_out, layer, *, tm):
    t, d = x.shape
    nt = t // tm
    nb = d // WEIGHT_COLS
    return pl.pallas_call(
        functools.partial(_gmlp_out_kernel, n_blocks=nb),
        grid=(nt,),
        in_specs=[
            pl.BlockSpec((tm, d), lambda i: (i, 0)),
            _resident((tm, d), lambda i: (0, 1)),
            pl.BlockSpec((tm, d), lambda i: (jnp.minimum(i + 1, nt - 1), 1)),
            _resident((None, 1, d), lambda i: (layer, 0, 0)),
            _resident((None, A_GROUPS, CHUNK, CHUNK), lambda i: (layer, 0, 0, 0)),
            _resident((None, CHUNK, A_GROUPS), lambda i: (layer, 0, 0)),
        ] + _weight_block_specs(d, d, layer) + [
            pl.BlockSpec((tm, d), lambda i: (i, 0)),
        ],
        out_specs=pl.BlockSpec((tm, d), lambda i: (i, 0)),
        out_shape=jax.ShapeDtypeStruct((t, d), F32),
        scratch_shapes=[pltpu.VMEM((tm, d), BF16)] * 3,
        compiler_params=_params("arbitrary"),
        name="gmlp_out",
    )(z, z, z, v_norm, w_s, b_s_t, *([w_out] * nb), x)


def _relative_bucket(rel):
    half = N_BUCKETS // 2
    max_exact = half // 2
    ret = (rel > 0).astype(np.int32) * half
    n = np.abs(rel)
    nf = np.maximum(n, 1).astype(np.float32)
    large = max_exact + (np.log(nf / max_exact) / np.log(MAX_DISTANCE / max_exact)
                         * (half - max_exact)).astype(np.int32)
    large = np.minimum(large, half - 1)
    return (ret + np.where(n < max_exact, n, large)).astype(np.int32)


def _band_bucket():
    a = np.arange(BLOCK)[:, None]
    c = np.arange(3 * BLOCK)[None, :]
    rel = c - BLOCK - a
    return np.where(np.abs(rel) <= WINDOW, _relative_bucket(rel), -1).astype(np.int32)


def _bias_table_kernel(bucket_ref, rb_ref, o_ref):
    bucket = bucket_ref[...]
    for h in range(N_HEADS):
        acc = jnp.full(bucket.shape, -jnp.inf, F32)
        for b in range(N_BUCKETS):
            acc = jnp.where(bucket == b, rb_ref[b, h] * LOG2E, acc)
        o_ref[h] = acc


def _bias_table(rel_bias):
    return pl.pallas_call(
        _bias_table_kernel,
        in_specs=[pl.BlockSpec(memory_space=pltpu.VMEM),
                  pl.BlockSpec(memory_space=pltpu.SMEM)],
        out_specs=pl.BlockSpec(memory_space=pltpu.VMEM),
        out_shape=jax.ShapeDtypeStruct((N_HEADS, BLOCK, 3 * BLOCK), F32),
        name="bias_table",
    )(jnp.asarray(_band_bucket()), rel_bias)


def _attn_kernel(q_ref, kp_ref, kc_ref, kn_ref, vp_ref, vc_ref, vn_ref, bias_ref, sink_ref,
                 o_ref, *, tiles_per_seq, layer):
    pos = pl.program_id(0) % tiles_per_seq
    before_seq = pos == 0
    after_seq = pos == tiles_per_seq - 1
    scale = HEAD_DIM ** -0.5 * LOG2E

    def head_cols(h):
        return slice(h * HEAD_DIM, (h + 1) * HEAD_DIM)

    def block_rows(s, n=1):
        return slice(s * BLOCK, (s + n) * BLOCK)

    def qk(s, kh):
        kc = head_cols(kh)
        k = jnp.concatenate([kp_ref[:, kc], kc_ref[:, kc], kn_ref[:, kc]], axis=0)
        q = jnp.concatenate([q_ref[block_rows(s), head_cols(kh * Q_PER_KV + g)]
                             for g in range(Q_PER_KV)], axis=0)
        return lax.dot_general(q, k[block_rows(s, 3)], (((1,), (1,)), ((), ())),
                               preferred_element_type=F32)

    def mask_outside(s, lg):
        pieces = [lg[:, block_rows(j)] for j in range(3)]
        if s == 0:
            pieces[0] = jnp.where(before_seq, -jnp.inf, pieces[0])
        if s == Q_BLOCKS - 1:
            pieces[2] = jnp.where(after_seq, -jnp.inf, pieces[2])
        return jnp.concatenate(pieces, axis=1)

    def softmax_pv(s, kh, logits):
        probs = []
        for g in range(Q_PER_KV):
            h = kh * Q_PER_KV + g
            lg = mask_outside(s, logits[block_rows(g)] * scale + bias_ref[h])
            sink = sink_ref[layer, h] * LOG2E
            m = jnp.maximum(jnp.max(lg, axis=-1, keepdims=True), sink)
            p = jnp.exp2(lg - m)
            denom = jnp.sum(p, axis=-1, keepdims=True) + jnp.exp2(sink - m)
            probs.append((p / denom).astype(BF16))
        kc = head_cols(kh)
        v = jnp.concatenate([vp_ref[:, kc], vc_ref[:, kc], vn_ref[:, kc]], axis=0)
        o = jnp.dot(jnp.concatenate(probs, axis=0), v[block_rows(s, 3)],
                    preferred_element_type=F32)
        for g in range(Q_PER_KV):
            o_ref[block_rows(s), head_cols(kh * Q_PER_KV + g)] = o[block_rows(g)].astype(o_ref.dtype)

    units = [(s, kh) for s in range(Q_BLOCKS) for kh in range(N_KV_HEADS)]
    logits = qk(*units[0])
    for n, unit in enumerate(units):
        nxt = qk(*units[n + 1]) if n + 1 < len(units) else None
        softmax_pv(*unit, logits)
        logits = nxt


def _attention(qkv, bias, sink, layer, seq_len):
    t = qkv.shape[0]
    nblk = t // BLOCK
    tq = Q_BLOCKS * BLOCK
    k0 = D_MODEL // KV_DIM
    v0 = k0 + 1

    def kv_specs(col):
        return [
            pl.BlockSpec((BLOCK, KV_DIM), lambda i: (jnp.maximum(i * Q_BLOCKS - 1, 0), col)),
            pl.BlockSpec((tq, KV_DIM), lambda i: (i, col)),
            pl.BlockSpec((BLOCK, KV_DIM),
                         lambda i: (jnp.minimum((i + 1) * Q_BLOCKS, nblk - 1), col)),
        ]

    return pl.pallas_call(
        functools.partial(_attn_kernel, tiles_per_seq=seq_len // tq, layer=layer),
        grid=(t // tq,),
        in_specs=[pl.BlockSpec((tq, D_MODEL), lambda i: (i, 0))]
        + kv_specs(k0) + kv_specs(v0)
        + [_resident((N_HEADS, BLOCK, 3 * BLOCK), lambda i: (0, 0, 0)),
           pl.BlockSpec(memory_space=pltpu.SMEM)],
        out_specs=pl.BlockSpec((tq, D_MODEL), lambda i: (i, 0)),
        out_shape=jax.ShapeDtypeStruct((t, D_MODEL), BF16),
        compiler_params=_params("parallel"),
        name="band_attention",
    )(qkv, qkv, qkv, qkv, qkv, qkv, qkv, bias, sink)


def _proj_residual_kernel(a_ref, *refs, n_blocks):
    x_ref, o_ref = refs[n_blocks:]
    _proj_residual_blocks(a_ref, refs[:n_blocks], x_ref, o_ref)


def _proj_residual(a, w, layer, x, *, tm):
    t, d = x.shape
    k = a.shape[1]
    nb = d // WEIGHT_COLS
    return pl.pallas_call(
        functools.partial(_proj_residual_kernel, n_blocks=nb),
        grid=(t // tm,),
        in_specs=[pl.BlockSpec((tm, k), lambda i: (i, 0))]
        + _weight_block_specs(k, d, layer)
        + [pl.BlockSpec((tm, d), lambda i: (i, 0))],
        out_specs=pl.BlockSpec((tm, d), lambda i: (i, 0)),
        out_shape=jax.ShapeDtypeStruct((t, d), F32),
        compiler_params=_params("parallel"),
        name="proj_residual",
    )(a, *([w] * nb), x)


def _ffn_kernel(x_ref, xp_ref, xn_ref, g_ref, wg_ref, wu_ref, cwg_ref, cwu_ref, cbg_ref,
                cbu_ref, wo_ref, gf_ref, o_ref, h_ref, *z_refs, tiles_per_seq, final_norm):
    i = pl.program_id(0)
    c = pl.program_id(1)
    tm = x_ref.shape[0]
    tc = wg_ref.shape[1]
    n_parts = len(z_refs) // 2
    tp = tc // n_parts

    @pl.when(c == 0)
    def _():
        _rmsnorm_rows(x_ref, g_ref, h_ref, HALO, tm)
        pos = i % tiles_per_seq
        hp = jnp.where(pos == 0, 0.0, _rmsnorm_val(xp_ref[...], g_ref[...]))
        hn = jnp.where(pos == tiles_per_seq - 1, 0.0, _rmsnorm_val(xn_ref[...], g_ref[...]))
        halo_row = lax.broadcasted_iota(jnp.int32, (HALO, 1), 0)
        h_ref[0:HALO, :] = jnp.where(halo_row == 0, hn, hp).astype(BF16)

    part_cols = [slice(p * tp, (p + 1) * tp) for p in range(n_parts)]

    def up_proj(z_ref, w_ref, cols):
        z_ref[0:HALO + tm, :] = jnp.dot(h_ref[...], w_ref[:, cols], preferred_element_type=F32)
        z_ref[HALO + tm:HALO + tm + F32_SUBLANES, :] = z_ref[0:F32_SUBLANES, :]

    def conv(z_ref, cw_ref, cb_ref, cols):
        return (z_ref[pl.ds(HALO - 1, tm), :] * cw_ref[0:1, cols]
                + z_ref[pl.ds(HALO, tm), :] * cw_ref[1:2, cols]
                + z_ref[pl.ds(HALO + 1, tm), :] * cw_ref[2:3, cols]
                + cb_ref[:, cols])

    def chunk(first):
        for p, cols in enumerate(part_cols):
            up_proj(z_refs[2 * p], wg_ref, cols)
        for p, cols in enumerate(part_cols):
            up_proj(z_refs[2 * p + 1], wu_ref, cols)
        gates = [jax.nn.silu(conv(z_refs[2 * p], cwg_ref, cbg_ref, cols))
                 for p, cols in enumerate(part_cols)]
        for p, cols in enumerate(part_cols):
            up = conv(z_refs[2 * p + 1], cwu_ref, cbu_ref, cols)
            act = (gates[p] * up).astype(BF16)
            d = jnp.dot(act, wo_ref[cols, :], preferred_element_type=F32)
            if first and p == 0:
                o_ref[...] = x_ref[...] + d
            else:
                o_ref[...] += d

    @pl.when(c == 0)
    def _():
        chunk(True)

    @pl.when(c > 0)
    def _():
        chunk(False)

    if final_norm:
        @pl.when(c == pl.num_programs(1) - 1)
        def _():
            _rmsnorm_rows(o_ref, gf_ref, o_ref, 0, tm)


def _ffn(x, gain, w_in, conv_w, conv_b, w_out, final_gain, layer, seq_len, *, tm, tc,
         final_norm):
    t, d = x.shape
    nch = D_FF // tc
    hb = tm // HALO
    nhb = t // HALO
    return pl.pallas_call(
        functools.partial(_ffn_kernel, tiles_per_seq=seq_len // tm, final_norm=final_norm),
        grid=(t // tm, nch),
        in_specs=[
            pl.BlockSpec((tm, d), lambda i, c: (i, 0)),
            pl.BlockSpec((HALO, d), lambda i, c: (jnp.maximum(i * hb - 1, 0), 0)),
            pl.BlockSpec((HALO, d), lambda i, c: (jnp.minimum((i + 1) * hb, nhb - 1), 0)),
            _resident((None, 1, d), lambda i, c: (layer, 0, 0)),
            pl.BlockSpec((None, d, tc), lambda i, c: (layer, 0, c)),
            pl.BlockSpec((None, d, tc), lambda i, c: (layer, 0, nch + c)),
            pl.BlockSpec((None, 3, tc), lambda i, c: (layer, 0, c)),
            pl.BlockSpec((None, 3, tc), lambda i, c: (layer, 0, nch + c)),
            pl.BlockSpec((None, 1, tc), lambda i, c: (layer, 0, c)),
            pl.BlockSpec((None, 1, tc), lambda i, c: (layer, 0, nch + c)),
            pl.BlockSpec((None, tc, d), lambda i, c: (layer, c, 0)),
            _resident((1, d), lambda i, c: (0, 0)),
        ],
        out_specs=pl.BlockSpec((tm, d), lambda i, c: (i, 0)),
        out_shape=jax.ShapeDtypeStruct((t, d), F32),
        scratch_shapes=[pltpu.VMEM((tm + HALO, d), BF16)]
        + [pltpu.VMEM((tm + HALO + F32_SUBLANES, tc // FFN_PARTS), F32)] * (2 * FFN_PARTS),
        compiler_params=_params("parallel", "arbitrary"),
        name="conv_ffn",
    )(x, x, x, gain, w_in, w_in, conv_w, conv_w, conv_b, conv_b, w_out, final_gain)


def _trunk(x3, p):
    b, s, d = x3.shape
    x = x3.reshape(b * s, d)
    for i in range(DEPTH):
        j = i // 2
        if i % 2 == 0:
            z = _norm_proj(x, p["mix_norm"], i, p["a_w_in"], j, p["a_b_in"],
                           out_dtype=F32, tm=512)
            x = _gmlp_out(z, x, p["a_v_norm"], p["a_w_s"], p["a_b_s_t"], p["a_w_out"], j, tm=512)
        else:
            qkv = _norm_proj(x, p["mix_norm"], i, p["b_w_qkv"], j, None,
                             out_dtype=BF16, tm=512)
            o = _attention(qkv, p["bias"], p["b_sink"], j, s)
            x = _proj_residual(o, p["b_w_out"], j, x, tm=512)
        x = _ffn(x, p["ffn_norm"], p["f_w_in"], p["f_conv_w"], p["f_conv_b"], p["f_w_out"],
                 p["final_norm"], i, s, tm=1024, tc=512, final_norm=i == DEPTH - 1)
    return x.reshape(b, s, d)


def kernel(x_prompt, x_sample, rel_bias, mix_norm, ffn_norm, final_norm, a_w_in, a_b_in, a_v_norm, a_w_s, a_b_s, a_w_out, b_w_qkv, b_sink, b_w_out, f_w_in, f_conv_w, f_conv_b, f_w_out):
    row = lambda a: a[:, None, :]
    p = {
        "mix_norm": row(mix_norm), "ffn_norm": row(ffn_norm), "final_norm": final_norm[None, :],
        "a_w_in": a_w_in.astype(BF16), "a_b_in": row(a_b_in), "a_v_norm": row(a_v_norm),
        "a_w_s": a_w_s.astype(BF16), "a_b_s_t": jnp.swapaxes(a_b_s, 1, 2),
        "a_w_out": a_w_out.astype(BF16),
        "b_w_qkv": b_w_qkv.astype(BF16), "b_sink": b_sink, "b_w_out": b_w_out.astype(BF16),
        "f_w_in": f_w_in.astype(BF16), "f_conv_w": f_conv_w, "f_conv_b": row(f_conv_b),
        "f_w_out": f_w_out.astype(BF16),
        "bias": _bias_table(rel_bias),
    }
    return (_trunk(x_prompt, p), _trunk(x_sample, p))
```

```python
import functools

import numpy as np
import jax
import jax.numpy as jnp
from jax import lax
from jax.experimental import pallas as pl
from jax.experimental.pallas import tpu as pltpu

F32 = jnp.float32
BF16 = jnp.bfloat16

D_MODEL = 2048
DEPTH = 4
CHUNK = 128
A_GROUPS = 8
A_GROUP_DIM = D_MODEL // A_GROUPS
HEAD_DIM = 128
N_HEADS = D_MODEL // HEAD_DIM
N_KV_HEADS = 4
Q_PER_KV = N_HEADS // N_KV_HEADS
KV_DIM = N_KV_HEADS * HEAD_DIM
WINDOW = 128
BLOCK = 128
Q_BLOCKS = 4
N_BUCKETS = 32
MAX_DISTANCE = 128
D_FF = 5632
EPS = 1e-6
LOG2E = 1.4426950408889634

VMEM_LIMIT_BYTES = 60 * 1024 * 1024
F32_SUBLANES = 8
BF16_SUBLANES = 16
HALO = BF16_SUBLANES
NORM_ROWS = 64
WEIGHT_COLS = 512
FFN_PARTS = 2


def _params(*sem):
    return pltpu.CompilerParams(dimension_semantics=sem,
                                vmem_limit_bytes=VMEM_LIMIT_BYTES)


def _resident(block_shape, index_map):
    return pl.BlockSpec(block_shape, index_map, pipeline_mode=pl.Buffered(1))


def _rmsnorm_val(x, g):
    ms = jnp.mean(x * x, axis=-1, keepdims=True)
    return (x * lax.rsqrt(ms + EPS)) * g


def _rmsnorm_rows(x_ref, g_ref, dst_ref, dst_off, n_rows):
    g = g_ref[...]

    def body(c, carry):
        r = pl.multiple_of(c * NORM_ROWS, NORM_ROWS)
        x = x_ref[pl.ds(r, NORM_ROWS), :]
        dst_ref[pl.ds(dst_off + r, NORM_ROWS), :] = _rmsnorm_val(x, g).astype(dst_ref.dtype)
        return carry

    lax.fori_loop(0, n_rows // NORM_ROWS, body, 0, unroll=4)


def _rmsnorm_rows_inline(x_ref, g_ref, dst_ref, n_rows):
    g = g_ref[...]
    for c in range(n_rows // NORM_ROWS):
        rows = slice(c * NORM_ROWS, (c + 1) * NORM_ROWS)
        dst_ref[rows, :] = _rmsnorm_val(x_ref[rows, :], g).astype(dst_ref.dtype)


def _norm_proj_kernel(x0_ref, xn_ref, g_ref, *refs, n_blocks, gelu):
    w_refs = refs[:n_blocks]
    b_ref = refs[n_blocks] if gelu else None
    if gelu:
        o_ref, h0_ref, h1_ref, raw0_ref, raw1_ref = refs[-5:]
        raw_refs = (raw0_ref, raw1_ref)
    else:
        o_ref, h0_ref, h1_ref = refs[-3:]
    i = pl.program_id(0)
    tm = x0_ref.shape[0]

    def block_cols(n):
        return slice(n * WEIGHT_COLS, (n + 1) * WEIGHT_COLS)

    def activate(n):
        o_ref[:, block_cols(n)] = jax.nn.gelu(
            raw_refs[n % 2][...] + b_ref[:, block_cols(n)]).astype(o_ref.dtype)

    @pl.when(i == 0)
    def _():
        _rmsnorm_rows(x0_ref, g_ref, h0_ref, 0, tm)

    for parity, (cur_ref, nxt_ref) in enumerate(((h0_ref, h1_ref), (h1_ref, h0_ref))):
        @pl.when(i % 2 == parity)
        def _():
            _rmsnorm_rows_inline(xn_ref, g_ref, nxt_ref, tm)
            for n, w_ref in enumerate(w_refs):
                acc = jnp.dot(cur_ref[...], w_ref[...], preferred_element_type=F32)
                if gelu:
                    raw_refs[n % 2][...] = acc
                    if n > 0:
                        activate(n - 1)
                else:
                    o_ref[:, block_cols(n)] = acc.astype(o_ref.dtype)
            if gelu:
                activate(n_blocks - 1)


def _weight_block_specs(k, n, layer):
    return [_resident((None, k, WEIGHT_COLS), lambda *_, j=j: (layer, 0, j))
            for j in range(n // WEIGHT_COLS)]


def _proj_residual_blocks(a_ref, w_refs, x_ref, o_ref):
    for n, w_ref in enumerate(w_refs):
        cols = slice(n * WEIGHT_COLS, (n + 1) * WEIGHT_COLS)
        o_ref[:, cols] = x_ref[:, cols] + jnp.dot(a_ref[...], w_ref[...],
                                                  preferred_element_type=F32)


def _norm_proj(x, gain, layer_g, w, layer_w, bias, *, out_dtype, tm):
    t, d = x.shape
    n = w.shape[-1]
    nt = t // tm
    nb = n // WEIGHT_COLS
    in_specs = [
        _resident((tm, d), lambda i: (0, 0)),
        pl.BlockSpec((tm, d), lambda i: (jnp.minimum(i + 1, nt - 1), 0)),
        _resident((None, 1, d), lambda i: (layer_g, 0, 0)),
    ] + _weight_block_specs(d, n, layer_w)
    args = [x, x, gain] + [w] * nb
    if bias is not None:
        in_specs.append(_resident((None, 1, n), lambda i: (layer_w, 0, 0)))
        args.append(bias)
    return pl.pallas_call(
        functools.partial(_norm_proj_kernel, n_blocks=nb, gelu=bias is not None),
        grid=(nt,),
        in_specs=in_specs,
        out_specs=pl.BlockSpec((tm, n), lambda i: (i, 0)),
        out_shape=jax.ShapeDtypeStruct((t, n), out_dtype),
        scratch_shapes=[pltpu.VMEM((tm, d), BF16)] * 2
        + ([] if bias is None else [pltpu.VMEM((tm, WEIGHT_COLS), F32)] * 2),
        compiler_params=_params("arbitrary"),
        name="norm_proj" if bias is None else "norm_proj_gelu",
    )(*args)


def _gmlp_out_kernel(u_ref, v0_ref, vnext_ref, vg_ref, ws_ref, bs_ref, *refs, n_blocks):
    wo_refs = refs[:n_blocks]
    x_ref, o_ref, vn0_ref, vn1_ref, y_ref = refs[n_blocks:]
    i = pl.program_id(0)
    tm = u_ref.shape[0]

    @pl.when(i == 0)
    def _():
        _rmsnorm_rows(v0_ref, vg_ref, vn0_ref, 0, tm)

    for parity, (cur_ref, nxt_ref) in enumerate(((vn0_ref, vn1_ref), (vn1_ref, vn0_ref))):
        @pl.when(i % 2 == parity)
        def _():
            _rmsnorm_rows_inline(vnext_ref, vg_ref, nxt_ref, tm)
            for n in range(tm // CHUNK):
                rows = slice(n * CHUNK, (n + 1) * CHUNK)
                for g in range(A_GROUPS):
                    cols = slice(g * A_GROUP_DIM, (g + 1) * A_GROUP_DIM)
                    s = jnp.dot(ws_ref[g], cur_ref[rows, cols], preferred_element_type=F32)
                    s = s + bs_ref[:, g:g + 1]
                    y_ref[rows, cols] = (u_ref[rows, cols] * s).astype(BF16)
            _proj_residual_blocks(y_ref, wo_refs, x_ref, o_ref)


def _gmlp_out(z, x, v_norm, w_s, b_s_t, w_out, layer, *, tm):
    t, d = x.shape
    nt = t // tm
    nb = d // WEIGHT_COLS
    return pl.pallas_call(
        functools.partial(_gmlp_out_kernel, n_blocks=nb),
        grid=(nt,),
        in_specs=[
            pl.BlockSpec((tm, d), lambda i: (i, 0)),
            _resident((tm, d), lambda i: (0, 1)),
            pl.BlockSpec((tm, d), lambda i: (jnp.minimum(i + 1, nt - 1), 1)),
            _resident((None, 1, d), lambda i: (layer, 0, 0)),
            _resident((None, A_GROUPS, CHUNK, CHUNK), lambda i: (layer, 0, 0, 0)),
            _resident((None, CHUNK, A_GROUPS), lambda i: (layer, 0, 0)),
        ] + _weight_block_specs(d, d, layer) + [
            pl.BlockSpec((tm, d), lambda i: (i, 0)),
        ],
        out_specs=pl.BlockSpec((tm, d), lambda i: (i, 0)),
        out_shape=jax.ShapeDtypeStruct((t, d), F32),
        scratch_shapes=[pltpu.VMEM((tm, d), BF16)] * 3,
        compiler_params=_params("arbitrary"),
        name="gmlp_out",
    )(z, z, z, v_norm, w_s, b_s_t, *([w_out] * nb), x)


def _relative_bucket(rel):
    half = N_BUCKETS // 2
    max_exact = half // 2
    ret = (rel > 0).astype(np.int32) * half
    n = np.abs(rel)
    nf = np.maximum(n, 1).astype(np.float32)
    large = max_exact + (np.log(nf / max_exact) / np.log(MAX_DISTANCE / max_exact)
                         * (half - max_exact)).astype(np.int32)
    large = np.minimum(large, half - 1)
    return (ret + np.where(n < max_exact, n, large)).astype(np.int32)


def _band_bucket():
    a = np.arange(BLOCK)[:, None]
    c = np.arange(3 * BLOCK)[None, :]
    rel = c - BLOCK - a
    return np.where(np.abs(rel) <= WINDOW, _relative_bucket(rel), -1).astype(np.int32)


def _bias_table_kernel(bucket_ref, rb_ref, o_ref):
    bucket = bucket_ref[...]
    for h in range(N_HEADS):
        acc = jnp.full(bucket.shape, -jnp.inf, F32)
        for b in range(N_BUCKETS):
            acc = jnp.where(bucket == b, rb_ref[b, h] * LOG2E, acc)
        o_ref[h] = acc


def _bias_table(rel_bias):
    return pl.pallas_call(
        _bias_table_kernel,
        in_specs=[pl.BlockSpec(memory_space=pltpu.VMEM),
                  pl.BlockSpec(memory_space=pltpu.SMEM)],
        out_specs=pl.BlockSpec(memory_space=pltpu.VMEM),
        out_shape=jax.ShapeDtypeStruct((N_HEADS, BLOCK, 3 * BLOCK), F32),
        name="bias_table",
    )(jnp.asarray(_band_bucket()), rel_bias)


def _attn_kernel(q_ref, kp_ref, kc_ref, kn_ref, vp_ref, vc_ref, vn_ref, bias_ref, sink_ref,
                 o_ref, *, tiles_per_seq, layer):
    pos = pl.program_id(0) % tiles_per_seq
    before_seq = pos == 0
    after_seq = pos == tiles_per_seq - 1
    scale = HEAD_DIM ** -0.5 * LOG2E

    def head_cols(h):
        return slice(h * HEAD_DIM, (h + 1) * HEAD_DIM)

    def block_rows(s, n=1):
        return slice(s * BLOCK, (s + n) * BLOCK)

    def qk(s, kh):
        kc = head_cols(kh)
        k = jnp.concatenate([kp_ref[:, kc], kc_ref[:, kc], kn_ref[:, kc]], axis=0)
        q = jnp.concatenate([q_ref[block_rows(s), head_cols(kh * Q_PER_KV + g)]
                             for g in range(Q_PER_KV)], axis=0)
        return lax.dot_general(q, k[block_rows(s, 3)], (((1,), (1,)), ((), ())),
                               preferred_element_type=F32)

    def mask_outside(s, lg):
        pieces = [lg[:, block_rows(j)] for j in range(3)]
        if s == 0:
            pieces[0] = jnp.where(before_seq, -jnp.inf, pieces[0])
        if s == Q_BLOCKS - 1:
            pieces[2] = jnp.where(after_seq, -jnp.inf, pieces[2])
        return jnp.concatenate(pieces, axis=1)

    def softmax_pv(s, kh, logits):
        probs = []
        for g in range(Q_PER_KV):
            h = kh * Q_PER_KV + g
            lg = mask_outside(s, logits[block_rows(g)] * scale + bias_ref[h])
            sink = sink_ref[layer, h] * LOG2E
            m = jnp.maximum(jnp.max(lg, axis=-1, keepdims=True), sink)
            p = jnp.exp2(lg - m)
            denom = jnp.sum(p, axis=-1, keepdims=True) + jnp.exp2(sink - m)
            probs.append((p / denom).astype(BF16))
        kc = head_cols(kh)
        v = jnp.concatenate([vp_ref[:, kc], vc_ref[:, kc], vn_ref[:, kc]], axis=0)
        o = jnp.dot(jnp.concatenate(probs, axis=0), v[block_rows(s, 3)],
                    preferred_element_type=F32)
        for g in range(Q_PER_KV):
            o_ref[block_rows(s), head_cols(kh * Q_PER_KV + g)] = o[block_rows(g)].astype(o_ref.dtype)

    units = [(s, kh) for s in range(Q_BLOCKS) for kh in range(N_KV_HEADS)]
    logits = qk(*units[0])
    for n, unit in enumerate(units):
        nxt = qk(*units[n + 1]) if n + 1 < len(units) else None
        softmax_pv(*unit, logits)
        logits = nxt


def _attention(qkv, bias, sink, layer, seq_len):
    t = qkv.shape[0]
    nblk = t // BLOCK
    tq = Q_BLOCKS * BLOCK
    k0 = D_MODEL // KV_DIM
    v0 = k0 + 1

    def kv_specs(col):
        return [
            pl.BlockSpec((BLOCK, KV_DIM), lambda i: (jnp.maximum(i * Q_BLOCKS - 1, 0), col)),
            pl.BlockSpec((tq, KV_DIM), lambda i: (i, col)),
            pl.BlockSpec((BLOCK, KV_DIM),
                         lambda i: (jnp.minimum((i + 1) * Q_BLOCKS, nblk - 1), col)),
        ]

    return pl.pallas_call(
        functools.partial(_attn_kernel, tiles_per_seq=seq_len // tq, layer=layer),
        grid=(t // tq,),
        in_specs=[pl.BlockSpec((tq, D_MODEL), lambda i: (i, 0))]
        + kv_specs(k0) + kv_specs(v0)
        + [_resident((N_HEADS, BLOCK, 3 * BLOCK), lambda i: (0, 0, 0)),
           pl.BlockSpec(memory_space=pltpu.SMEM)],
        out_specs=pl.BlockSpec((tq, D_MODEL), lambda i: (i, 0)),
        out_shape=jax.ShapeDtypeStruct((t, D_MODEL), BF16),
        compiler_params=_params("parallel"),
        name="band_attention",
    )(qkv, qkv, qkv, qkv, qkv, qkv, qkv, bias, sink)


def _proj_residual_kernel(a_ref, *refs, n_blocks):
    x_ref, o_ref = refs[n_blocks:]
    _proj_residual_blocks(a_ref, refs[:n_blocks], x_ref, o_ref)


def _proj_residual(a, w, layer, x, *, tm):
    t, d = x.shape
    k = a.shape[1]
    nb = d // WEIGHT_COLS
    return pl.pallas_call(
        functools.partial(_proj_residual_kernel, n_blocks=nb),
        grid=(t // tm,),
        in_specs=[pl.BlockSpec((tm, k), lambda i: (i, 0))]
        + _weight_block_specs(k, d, layer)
        + [pl.BlockSpec((tm, d), lambda i: (i, 0))],
        out_specs=pl.BlockSpec((tm, d), lambda i: (i, 0)),
        out_shape=jax.ShapeDtypeStruct((t, d), F32),
        compiler_params=_params("parallel"),
        name="proj_residual",
    )(a, *([w] * nb), x)


def _ffn_kernel(x_ref, xp_ref, xn_ref, g_ref, wg_ref, wu_ref, cwg_ref, cwu_ref, cbg_ref,
                cbu_ref, wo_ref, gf_ref, o_ref, h_ref, *z_refs, tiles_per_seq, final_norm):
    i = pl.program_id(0)
    c = pl.program_id(1)
    tm = x_ref.shape[0]
    tc = wg_ref.shape[1]
    n_parts = len(z_refs) // 2
    tp = tc // n_parts

    @pl.when(c == 0)
    def _():
        _rmsnorm_rows(x_ref, g_ref, h_ref, HALO, tm)
        pos = i % tiles_per_seq
        hp = jnp.where(pos == 0, 0.0, _rmsnorm_val(xp_ref[...], g_ref[...]))
        hn = jnp.where(pos == tiles_per_seq - 1, 0.0, _rmsnorm_val(xn_ref[...], g_ref[...]))
        halo_row = lax.broadcasted_iota(jnp.int32, (HALO, 1), 0)
        h_ref[0:HALO, :] = jnp.where(halo_row == 0, hn, hp).astype(BF16)

    part_cols = [slice(p * tp, (p + 1) * tp) for p in range(n_parts)]

    def up_proj(z_ref, w_ref, cols):
        z_ref[0:HALO + tm, :] = jnp.dot(h_ref[...], w_ref[:, cols], preferred_element_type=F32)
        z_ref[HALO + tm:HALO + tm + F32_SUBLANES, :] = z_ref[0:F32_SUBLANES, :]

    def conv(z_ref, cw_ref, cb_ref, cols):
        return (z_ref[pl.ds(HALO - 1, tm), :] * cw_ref[0:1, cols]
                + z_ref[pl.ds(HALO, tm), :] * cw_ref[1:2, cols]
                + z_ref[pl.ds(HALO + 1, tm), :] * cw_ref[2:3, cols]
                + cb_ref[:, cols])

    def chunk(first):
        for p, cols in enumerate(part_cols):
            up_proj(z_refs[2 * p], wg_ref, cols)
        for p, cols in enumerate(part_cols):
            up_proj(z_refs[2 * p + 1], wu_ref, cols)
        gates = [jax.nn.silu(conv(z_refs[2 * p], cwg_ref, cbg_ref, cols))
                 for p, cols in enumerate(part_cols)]
        for p, cols in enumerate(part_cols):
            up = conv(z_refs[2 * p + 1], cwu_ref, cbu_ref, cols)
            act = (gates[p] * up).astype(BF16)
            d = jnp.dot(act, wo_ref[cols, :], preferred_element_type=F32)
            if first and p == 0:
                o_ref[...] = x_ref[...] + d
            else:
                o_ref[...] += d

    @pl.when(c == 0)
    def _():
        chunk(True)

    @pl.when(c > 0)
    def _():
        chunk(False)

    if final_norm:
        @pl.when(c == pl.num_programs(1) - 1)
        def _():
            _rmsnorm_rows(o_ref, gf_ref, o_ref, 0, tm)


def _ffn(x, gain, w_in, conv_w, conv_b, w_out, final_gain, layer, seq_len, *, tm, tc,
         final_norm):
    t, d = x.shape
    nch = D_FF // tc
    hb = tm // HALO
    nhb = t // HALO
    return pl.pallas_call(
        functools.partial(_ffn_kernel, tiles_per_seq=seq_len // tm, final_norm=final_norm),
        grid=(t // tm, nch),
        in_specs=[
            pl.BlockSpec((tm, d), lambda i, c: (i, 0)),
            pl.BlockSpec((HALO, d), lambda i, c: (jnp.maximum(i * hb - 1, 0), 0)),
            pl.BlockSpec((HALO, d), lambda i, c: (jnp.minimum((i + 1) * hb, nhb - 1), 0)),
            _resident((None, 1, d), lambda i, c: (layer, 0, 0)),
            pl.BlockSpec((None, d, tc), lambda i, c: (layer, 0, c)),
            pl.BlockSpec((None, d, tc), lambda i, c: (layer, 0, nch + c)),
            pl.BlockSpec((None, 3, tc), lambda i, c: (layer, 0, c)),
            pl.BlockSpec((None, 3, tc), lambda i, c: (layer, 0, nch + c)),
            pl.BlockSpec((None, 1, tc), lambda i, c: (layer, 0, c)),
            pl.BlockSpec((None, 1, tc), lambda i, c: (layer, 0, nch + c)),
            pl.BlockSpec((None, tc, d), lambda i, c: (layer, c, 0)),
            _resident((1, d), lambda i, c: (0, 0)),
        ],
        out_specs=pl.BlockSpec((tm, d), lambda i, c: (i, 0)),
        out_shape=jax.ShapeDtypeStruct((t, d), F32),
        scratch_shapes=[pltpu.VMEM((tm + HALO, d), BF16)]
        + [pltpu.VMEM((tm + HALO + F32_SUBLANES, tc // FFN_PARTS), F32)] * (2 * FFN_PARTS),
        compiler_params=_params("parallel", "arbitrary"),
        name="conv_ffn",
    )(x, x, x, gain, w_in, w_in, conv_w, conv_w, conv_b, conv_b, w_out, final_gain)


def _trunk(x3, p):
    b, s, d = x3.shape
    x = x3.reshape(b * s, d)
    for i in range(DEPTH):
        j = i // 2
        if i % 2 == 0:
            z = _norm_proj(x, p["mix_norm"], i, p["a_w_in"], j, p["a_b_in"],
                           out_dtype=F32, tm=512)
            x = _gmlp_out(z, x, p["a_v_norm"], p["a_w_s"], p["a_b_s_t"], p["a_w_out"], j, tm=512)
        else:
            qkv = _norm_proj(x, p["mix_norm"], i, p["b_w_qkv"], j, None,
                             out_dtype=BF16, tm=512)
            o = _attention(qkv, p["bias"], p["b_sink"], j, s)
            x = _proj_residual(o, p["b_w_out"], j, x, tm=512)
        x = _ffn(x, p["ffn_norm"], p["f_w_in"], p["f_conv_w"], p["f_conv_b"], p["f_w_out"],
                 p["final_norm"], i, s, tm=1024, tc=512, final_norm=i == DEPTH - 1)
    return x.reshape(b, s, d)


def kernel(x_prompt, x_sample, rel_bias, mix_norm, ffn_norm, final_norm, a_w_in, a_b_in, a_v_norm, a_w_s, a_b_s, a_w_out, b_w_qkv, b_sink, b_w_out, f_w_in, f_conv_w, f_conv_b, f_w_out):
    row = lambda a: a[:, None, :]
    p = {
        "mix_norm": row(mix_norm), "ffn_norm": row(ffn_norm), "final_norm": final_norm[None, :],
        "a_w_in": a_w_in.astype(BF16), "a_b_in": row(a_b_in), "a_v_norm": row(a_v_norm),
        "a_w_s": a_w_s.astype(BF16), "a_b_s_t": jnp.swapaxes(a_b_s, 1, 2),
        "a_w_out": a_w_out.astype(BF16),
        "b_w_qkv": b_w_qkv.astype(BF16), "b_sink": b_sink, "b_w_out": b_w_out.astype(BF16),
        "f_w_in": f_w_in.astype(BF16), "f_conv_w": f_conv_w, "f_conv_b": row(f_conv_b),
        "f_w_out": f_w_out.astype(BF16),
        "bias": _bias_table(rel_bias),
    }
    return (_trunk(x_prompt, p), _trunk(x_sample, p))
```
